```python
import math
import jax, jax.numpy as jnp
from jax import lax
import numpy as np

D_MODEL = 2048
BATCH = 4
SEQ = 4096
DEPTH = 1

N_DIFF_HEADS = 6
HEAD_DIM = 128
V_HEAD_DIM = 2 * HEAD_DIM
ATTN_QK_WIDTH = N_DIFF_HEADS * 2 * HEAD_DIM
ATTN_V_WIDTH = N_DIFF_HEADS * V_HEAD_DIM
LAYER_INDEX = 1
LAMBDA_INIT = 0.8 - 0.6 * math.exp(-0.3 * (LAYER_INDEX - 1))
Q_BLOCK = 128

ROPE_THETA = 500000.0
ROT_DIM = HEAD_DIM // 4

N_FOURIER_GROUPS = 4
FOURIER_GROUP_DIM = 128
FOURIER_WIDTH = N_FOURIER_GROUPS * FOURIER_GROUP_DIM

Q_END = ATTN_QK_WIDTH
K_END = Q_END + ATTN_QK_WIDTH
V_END = K_END + ATTN_V_WIDTH
F_END = V_END + FOURIER_WIDTH
GA_END = F_END + D_MODEL
GF_END = GA_END + D_MODEL
IN_WIDTH = GF_END

D_FF = ((8 * D_MODEL // 3 + 255) // 256) * 256

RMS_EPS = 1e-5

kernel_name = "gated_diffattn_fnet_hybrid_block"


def rmsnorm(x, g):
    xf = x.astype(jnp.float32)
    y = xf * lax.rsqrt(jnp.mean(xf * xf, axis=-1, keepdims=True) + RMS_EPS)
    return (y * g.astype(jnp.float32)).astype(x.dtype)


def rope_tables(seq_len, dtype):
    pos = jnp.arange(seq_len, dtype=jnp.float32)
    inv_freq = ROPE_THETA ** (-jnp.arange(0, ROT_DIM, 2, dtype=jnp.float32) / ROT_DIM)
    ang = pos[:, None] * inv_freq[None, :]
    cos = jnp.cos(ang)[None, :, None, None, :].astype(dtype)
    sin = jnp.sin(ang)[None, :, None, None, :].astype(dtype)
    return cos, sin


def partial_rope(x, cos, sin):
    xr, xp = x[..., :ROT_DIM], x[..., ROT_DIM:]
    x1, x2 = xr[..., : ROT_DIM // 2], xr[..., ROT_DIM // 2:]
    rot = jnp.concatenate([x1 * cos - x2 * sin, x2 * cos + x1 * sin], axis=-1)
    return jnp.concatenate([rot, xp], axis=-1)


def diff_attention(q, k, v, lam):
    b, s = q.shape[0], q.shape[1]
    n_blk = s // Q_BLOCK
    scale = 1.0 / math.sqrt(HEAD_DIM)
    qb = jnp.moveaxis(q.reshape(b, n_blk, Q_BLOCK, N_DIFF_HEADS, 2, HEAD_DIM), 1, 0)

    def one_block(qblk):
        sc = jnp.einsum('bqhcd,bkhcd->bhcqk', qblk, k).astype(jnp.float32) * scale
        p = jax.nn.softmax(sc, axis=-1)
        a = (p[:, :, 0] - lam * p[:, :, 1]).astype(v.dtype)
        return jnp.einsum('bhqk,bkhe->bqhe', a, v)

    o = lax.map(one_block, qb)
    return jnp.moveaxis(o, 0, 1).reshape(b, s, N_DIFF_HEADS, V_HEAD_DIM)


def fourier_mix(f):
    ff = jnp.fft.fft2(f.astype(jnp.float32), axes=(1, 3), norm="ortho")
    return jnp.real(ff).astype(f.dtype)


def setup_inputs(seed: int = 0) -> dict:
    key = jax.random.key(seed)
    ks = jax.random.split(key, 16)
    f32 = jnp.float32

    def dense(k, fan_in, fan_out):
        return jax.random.normal(k, (fan_in, fan_out), f32) * fan_in ** -0.5

    return {
        "x": jax.random.normal(ks[0], (BATCH, SEQ, D_MODEL), f32),
        "g_mix": 1.0 + 0.02 * jax.random.normal(ks[1], (D_MODEL,), f32),
        "w_in": dense(ks[2], D_MODEL, IN_WIDTH),
        "lambda_q1": 0.1 * jax.random.normal(ks[3], (HEAD_DIM,), f32),
        "lambda_k1": 0.1 * jax.random.normal(ks[4], (HEAD_DIM,), f32),
        "lambda_q2": 0.1 * jax.random.normal(ks[5], (HEAD_DIM,), f32),
        "lambda_k2": 0.1 * jax.random.normal(ks[6], (HEAD_DIM,), f32),
        "g_subln": 1.0 + 0.02 * jax.random.normal(ks[7], (V_HEAD_DIM,), f32),
        "w_attn_branch": dense(ks[8], ATTN_V_WIDTH, D_MODEL),
        "w_four_branch": dense(ks[9], FOURIER_WIDTH, D_MODEL),
        "w_out": dense(ks[10], D_MODEL, D_MODEL),
        "g_ffn": 1.0 + 0.02 * jax.random.normal(ks[11], (D_MODEL,), f32),
        "w_gate": dense(ks[12], D_MODEL, D_FF),
        "w_up": dense(ks[13], D_MODEL, D_FF),
        "w_down": dense(ks[14], D_FF, D_MODEL),
        "g_final": 1.0 + 0.02 * jax.random.normal(ks[15], (D_MODEL,), f32),
    }


def reference(x, g_mix, w_in, lambda_q1, lambda_k1, lambda_q2, lambda_k2, g_subln,
              w_attn_branch, w_four_branch, w_out, g_ffn, w_gate, w_up, w_down, g_final):
    b, s, _ = x.shape
    cos, sin = rope_tables(s, x.dtype)
    lam = (jnp.exp(jnp.sum(lambda_q1.astype(jnp.float32) * lambda_k1.astype(jnp.float32)))
           - jnp.exp(jnp.sum(lambda_q2.astype(jnp.float32) * lambda_k2.astype(jnp.float32)))
           + LAMBDA_INIT)

    for _ in range(DEPTH):
        h = rmsnorm(x, g_mix)
        z = h @ w_in
        q = z[..., :Q_END].reshape(b, s, N_DIFF_HEADS, 2, HEAD_DIM)
        k = z[..., Q_END:K_END].reshape(b, s, N_DIFF_HEADS, 2, HEAD_DIM)
        v = z[..., K_END:V_END].reshape(b, s, N_DIFF_HEADS, V_HEAD_DIM)
        f = z[..., V_END:F_END].reshape(b, s, N_FOURIER_GROUPS, FOURIER_GROUP_DIM)
        gate_a = jax.nn.sigmoid(z[..., F_END:GA_END])
        gate_f = jax.nn.sigmoid(z[..., GA_END:GF_END])

        q = partial_rope(q, cos, sin)
        k = partial_rope(k, cos, sin)
        o = diff_attention(q, k, v, lam)
        o = rmsnorm(o, g_subln) * (1.0 - LAMBDA_INIT)
        branch_a = o.reshape(b, s, ATTN_V_WIDTH) @ w_attn_branch

        fm = fourier_mix(f).reshape(b, s, FOURIER_WIDTH)
        branch_f = fm @ w_four_branch

        merged = gate_a * branch_a + gate_f * branch_f
        x = x + merged @ w_out

        h2 = rmsnorm(x, g_ffn)
        x = x + (jax.nn.silu(h2 @ w_gate) * (h2 @ w_up)) @ w_down

    return rmsnorm(x, g_final)
```

```python
import functools
import math

import jax
import jax.numpy as jnp
from jax import lax
from jax.experimental import pallas as pl
from jax.experimental.pallas import tpu as pltpu

F32 = jnp.float32
BF16 = jnp.bfloat16

HEAD_DIM = 128
ROT_DIM = HEAD_DIM // 4
ROPE_THETA = 500000.0
LAMBDA_INIT = 0.8 - 0.6 * math.exp(-0.3 * 0)
FOURIER_GROUP_DIM = 128
RMS_EPS = 1e-5
FFT_RADIX = 16

V7X_LANES = 128
V7X_SUBLANES = 8
V7X_VMEM_LIMIT_BYTES = 58 * 1024 * 1024


def _cparams(sem):
    return pltpu.CompilerParams(dimension_semantics=sem, vmem_limit_bytes=V7X_VMEM_LIMIT_BYTES)


def _in_proj_kernel(x_ref, g_ref, w_ref, cos_ref, sa_ref, sb_ref, z_ref, h_ref, *,
                    n_q_tiles, n_rope_tiles, first_gate_tile, q_scale):
    j = pl.program_id(1)

    @pl.when(j == 0)
    def _():
        x = x_ref[...]
        ms = jnp.mean(x * x, axis=-1, keepdims=True)
        h_ref[...] = (x * lax.rsqrt(ms + RMS_EPS) * g_ref[...]).astype(BF16)

    acc = jnp.dot(h_ref[...], w_ref[...], preferred_element_type=F32)
    tn = acc.shape[1]

    @pl.when(j < n_rope_tiles)
    def _():
        rep = tn // HEAD_DIM
        c = jnp.concatenate([cos_ref[...]] * rep, axis=1)
        sa = jnp.concatenate([sa_ref[...]] * rep, axis=1)
        sb = jnp.concatenate([sb_ref[...]] * rep, axis=1)
        a = acc * jnp.where(j < n_q_tiles, q_scale, 1.0).astype(F32)
        half = ROT_DIM // 2
        r = a * c + pltpu.roll(a, tn - half, 1) * sa + pltpu.roll(a, half, 1) * sb
        z_ref[...] = r.astype(BF16)

    @pl.when(jnp.logical_and(j >= n_rope_tiles, j < first_gate_tile))
    def _():
        z_ref[...] = acc.astype(BF16)

    @pl.when(j >= first_gate_tile)
    def _():
        z_ref[...] = jax.nn.sigmoid(acc).astype(BF16)


def _in_proj(x2, g_mix, w_in, rope_c, rope_sa, rope_sb, *, seq, qk_width, gate_start, tm, tn):
    n, d = x2.shape
    in_width = w_in.shape[1]
    assert n % tm == 0 and seq % tm == 0 and in_width % tn == 0
    assert qk_width % tn == 0 and gate_start % tn == 0 and tn % HEAD_DIM == 0
    s_tiles = seq // tm
    kern = functools.partial(
        _in_proj_kernel, n_q_tiles=qk_width // tn, n_rope_tiles=2 * qk_width // tn,
        first_gate_tile=gate_start // tn, q_scale=1.0 / math.sqrt(HEAD_DIM))
    return pl.pallas_call(
        kern,
        out_shape=jax.ShapeDtypeStruct((n, in_width), BF16),
        grid=(n // tm, in_width // tn),
        in_specs=[
            pl.BlockSpec((tm, d), lambda i, j: (i, 0)),
            pl.BlockSpec((1, d), lambda i, j: (0, 0)),
            pl.BlockSpec((d, tn), lambda i, j: (0, j)),
            pl.BlockSpec((tm, HEAD_DIM), lambda i, j: (i % s_tiles, 0)),
            pl.BlockSpec((tm, HEAD_DIM), lambda i, j: (i % s_tiles, 0)),
            pl.BlockSpec((tm, HEAD_DIM), lambda i, j: (i % s_tiles, 0)),
        ],
        out_specs=pl.BlockSpec((tm, tn), lambda i, j: (i, j)),
        scratch_shapes=[pltpu.VMEM((tm, d), BF16)],
        compiler_params=_cparams(("parallel", "arbitrary")),
        name="in_proj",
    )(x2, g_mix.reshape(1, d), w_in, rope_c, rope_sa, rope_sb)


def _attn_kernel(lamv_ref, g_ref, q_ref, k_ref, v_ref, o_ref):
    lv = lamv_ref[...]
    lam = (jnp.exp(jnp.sum(lv[0:1] * lv[1:2], axis=-1, keepdims=True))
           - jnp.exp(jnp.sum(lv[2:3] * lv[3:4], axis=-1, keepdims=True)) + LAMBDA_INIT)
    q = q_ref[0]
    k = k_ref[0]
    v = v_ref[0]
    parts = []
    for c in range(2):
        qc = q[:, c * HEAD_DIM:(c + 1) * HEAD_DIM]
        kc = k[:, c * HEAD_DIM:(c + 1) * HEAD_DIM]
        s = lax.dot_general(qc, kc, (((1,), (1,)), ((), ())), preferred_element_type=F32)
        m = jnp.max(s, axis=-1, keepdims=True)
        p = jnp.exp(s - m)
        l = jnp.sum(p, axis=-1, keepdims=True)
        coef = (1.0 / l) if c == 0 else (lam / l)
        parts.append(p * coef)
    a = (parts[0] - parts[1]).astype(BF16)
    o = jnp.dot(a, v, preferred_element_type=F32)
    ms = jnp.mean(o * o, axis=-1, keepdims=True)
    o = o * lax.rsqrt(ms + RMS_EPS) * (g_ref[...] * (1.0 - LAMBDA_INIT))
    o_ref[0] = o.astype(BF16)


def _attention(z3, lamv, g_subln, *, n_heads, tq):
    b, s, _ = z3.shape
    vd = 2 * HEAD_DIM
    assert s % tq == 0
    return pl.pallas_call(
        _attn_kernel,
        out_shape=jax.ShapeDtypeStruct((b, s, n_heads * vd), BF16),
        grid=(b, n_heads, s // tq),
        in_specs=[
            pl.BlockSpec((4, HEAD_DIM), lambda bi, h, qi: (0, 0)),
            pl.BlockSpec((1, vd), lambda bi, h, qi: (0, 0)),
            pl.BlockSpec((1, tq, vd), lambda bi, h, qi: (bi, qi, h)),
            pl.BlockSpec((1, s, vd), lambda bi, h, qi: (bi, 0, n_heads + h)),
            pl.BlockSpec((1, s, vd), lambda bi, h, qi: (bi, 0, 2 * n_heads + h)),
        ],
        out_specs=pl.BlockSpec((1, tq, vd), lambda bi, h, qi: (bi, qi, h)),
        compiler_params=_cparams(("parallel", "parallel", "arbitrary")),
        name="diff_attn",
    )(lamv, g_subln.reshape(1, vd), z3, z3, z3)


def _cmul_const(x, wr, wi):
    xr, xi = x
    wr = 0.0 if abs(wr) < 1e-12 else wr
    wi = 0.0 if abs(wi) < 1e-12 else wi
    if wi == 0.0:
        return x if wr == 1.0 else ((-xr, -xi) if wr == -1.0 else (xr * wr, xi * wr))
    if wr == 0.0:
        return (xi, -xr) if wi == -1.0 else ((-xi, xr) if wi == 1.0 else (-xi * wi, xr * wi))
    return (xr * wr - xi * wi, xr * wi + xi * wr)


def _fft_list(xs):
    n = len(xs)
    if n == 1:
        return xs
    ev = _fft_list(xs[0::2])
    od = _fft_list(xs[1::2])
    out = [None] * n
    for k in range(n // 2):
        ang = -2.0 * math.pi * k / n
        tr, ti = _cmul_const(od[k], math.cos(ang), math.sin(ang))
        out[k] = (ev[k][0] + tr, ev[k][1] + ti)
        out[k + n // 2] = (ev[k][0] - tr, ev[k][1] - ti)
    return out


def _fourier_kernel(f_ref, cd_ref, twc_ref, tws_ref, cs_ref, out_ref, g_ref):
    _, n1, n2, c = f_ref.shape
    n_groups = c // FOURIER_GROUP_DIM
    f2 = f_ref[0].reshape(n1 * n2, c)
    for grp in range(n_groups):
        lo = grp * FOURIER_GROUP_DIM
        gg = jnp.dot(f2[:, lo:lo + FOURIER_GROUP_DIM], cd_ref[...], preferred_element_type=F32)
        g_ref[:, :, lo:lo + FOURIER_GROUP_DIM] = gg[:, :FOURIER_GROUP_DIM].reshape(n1, n2, FOURIER_GROUP_DIM)
        g_ref[:, :, c + lo:c + lo + FOURIER_GROUP_DIM] = gg[:, FOURIER_GROUP_DIM:].reshape(n1, n2, FOURIER_GROUP_DIM)

    def body(t, carry):
        r = pl.multiple_of(t * V7X_SUBLANES, V7X_SUBLANES)
        rows = pl.ds(r, V7X_SUBLANES)
        for lc in range(c // V7X_LANES):
            re_l = slice(lc * V7X_LANES, (lc + 1) * V7X_LANES)
            im_l = slice(c + lc * V7X_LANES, c + (lc + 1) * V7X_LANES)
            ys = _fft_list([(g_ref[a, rows, re_l], g_ref[a, rows, im_l]) for a in range(n1)])
            for k1 in range(n1):
                yr, yi = ys[k1]
                tc = twc_ref[k1, rows, :]
                ts = tws_ref[k1, rows, :]
                g_ref[k1, rows, re_l] = yr * tc + yi * ts
                g_ref[k1, rows, im_l] = yi * tc - yr * ts
        return carry

    lax.fori_loop(0, n2 // V7X_SUBLANES, body, 0)

    cc = cs_ref[:, :n2]
    ss = cs_ref[:, n2:]
    for k1 in range(n1):
        zr = g_ref[k1, :, :c].astype(BF16)
        zi = g_ref[k1, :, c:].astype(BF16)
        xk = (jnp.dot(cc, zr, preferred_element_type=F32) + jnp.dot(ss, zi, preferred_element_type=F32))
        out_ref[0, :, k1 * c:(k1 + 1) * c] = xk.astype(BF16)


def _fourier(z4, cd, twc, tws, cs, *, f_col_block, c):
    b, n1, n2, _ = z4.shape
    return pl.pallas_call(
        _fourier_kernel,
        out_shape=jax.ShapeDtypeStruct((b, n2, n1 * c), BF16),
        grid=(b,),
        in_specs=[
            pl.BlockSpec((1, n1, n2, c), lambda bi: (bi, 0, 0, f_col_block)),
            pl.BlockSpec(cd.shape, lambda bi: (0, 0)),
            pl.BlockSpec(twc.shape, lambda bi: (0, 0, 0)),
            pl.BlockSpec(tws.shape, lambda bi: (0, 0, 0)),
            pl.BlockSpec(cs.shape, lambda bi: (0, 0)),
        ],
        out_specs=pl.BlockSpec((1, n2, n1 * c), lambda bi: (bi, 0, 0)),
        scratch_shapes=[pltpu.VMEM((n1, n2, 2 * c), F32)],
        compiler_params=_cparams(("parallel",)),
        name="fourier_mix",
    )(z4, cd, twc, tws, cs)


def _fourier_tables(seq):
    n1, n2 = FFT_RADIX, seq // FFT_RADIX
    gd = FOURIER_GROUP_DIM

    def cos_sin(n, scale):
        idx = jnp.arange(n, dtype=jnp.int32)
        ang = ((idx[:, None] * idx[None, :]) % n).astype(F32) * (2.0 * math.pi / n)
        return jnp.cos(ang) * scale, jnp.sin(ang) * scale

    cg, sg = cos_sin(gd, gd ** -0.5)
    cd = jnp.concatenate([cg, -sg], axis=1).astype(BF16)
    c2, s2 = cos_sin(n2, seq ** -0.5)
    cs = jnp.concatenate([c2, s2], axis=1).astype(BF16)
    k1 = jnp.arange(n1, dtype=jnp.int32)[:, None]
    r = jnp.arange(n2, dtype=jnp.int32)[None, :]
    ang = (k1 * r).astype(F32) * (2.0 * math.pi / seq)
    twc = jnp.broadcast_to(jnp.cos(ang)[:, :, None], (n1, n2, V7X_LANES))
    tws = jnp.broadcast_to(jnp.sin(ang)[:, :, None], (n1, n2, V7X_LANES))
    return cd, twc, tws, cs


def _merge_kernel(o_ref, fm_ref, ga0_ref, ga1_ref, gf0_ref, gf1_ref, x_ref, wa_ref, wf_ref, wo_ref,
                  g_ref, x1_ref, h2_ref):
    o = o_ref[...]
    fm = fm_ref[...]
    acc = x_ref[...]
    half = ga0_ref.shape[1]
    for ci, (ga_ref, gf_ref) in enumerate(((ga0_ref, gf0_ref), (ga1_ref, gf1_ref))):
        cols = slice(ci * half, (ci + 1) * half)
        br_a = jnp.dot(o, wa_ref[:, cols], preferred_element_type=F32)
        br_f = jnp.dot(fm, wf_ref[:, cols], preferred_element_type=F32)
        m = ga_ref[...].astype(F32) * br_a + gf_ref[...].astype(F32) * br_f
        acc = acc + jnp.dot(m.astype(BF16), wo_ref[cols, :], preferred_element_type=F32)
    x1_ref[...] = acc
    ms = jnp.mean(acc * acc, axis=-1, keepdims=True)
    h2_ref[...] = (acc * lax.rsqrt(ms + RMS_EPS) * g_ref[...]).astype(BF16)


def _merge(o2, fm2, z2, x2, wa, wf, wo, g_ffn, *, gate_start, tm):
    n, d = x2.shape
    half = d // 2
    assert gate_start % half == 0 and n % tm == 0
    gb = gate_start // half
    const = lambda i: (0, 0)
    return pl.pallas_call(
        _merge_kernel,
        out_shape=(jax.ShapeDtypeStruct((n, d), F32), jax.ShapeDtypeStruct((n, d), BF16)),
        grid=(n // tm,),
        in_specs=[
            pl.BlockSpec((tm, o2.shape[1]), lambda i: (i, 0)),
            pl.BlockSpec((tm, fm2.shape[1]), lambda i: (i, 0)),
            pl.BlockSpec((tm, half), lambda i: (i, gb)),
            pl.BlockSpec((tm, half), lambda i: (i, gb + 1)),
            pl.BlockSpec((tm, half), lambda i: (i, gb + 2)),
            pl.BlockSpec((tm, half), lambda i: (i, gb + 3)),
            pl.BlockSpec((tm, d), lambda i: (i, 0)),
            pl.BlockSpec(wa.shape, const, pipeline_mode=pl.Buffered(1)),
            pl.BlockSpec(wf.shape, const, pipeline_mode=pl.Buffered(1)),
            pl.BlockSpec(wo.shape, const, pipeline_mode=pl.Buffered(1)),
            pl.BlockSpec((1, d), const),
        ],
        out_specs=(pl.BlockSpec((tm, d), lambda i: (i, 0)), pl.BlockSpec((tm, d), lambda i: (i, 0))),
        compiler_params=_cparams(("parallel",)),
        name="merge_out_proj",
    )(o2, fm2, z2, z2, z2, z2, x2, wa, wf, wo, g_ffn.reshape(1, d))


def _ffn_kernel(h_ref, x1_ref, wg_ref, wu_ref, wd_ref, g_ref, out_ref, acc_ref):
    j = pl.program_id(1)
    h = h_ref[...]
    gate = jnp.dot(h, wg_ref[...], preferred_element_type=F32)
    up = jnp.dot(h, wu_ref[...], preferred_element_type=F32)
    act = (gate * jax.nn.sigmoid(gate) * up).astype(BF16)
    part = jnp.dot(act, wd_ref[...], preferred_element_type=F32)

    @pl.when(j == 0)
    def _():
        acc_ref[...] = x1_ref[...] + part

    @pl.when(j > 0)
    def _():
        acc_ref[...] += part

    @pl.when(j == pl.num_programs(1) - 1)
    def _():
        y = acc_ref[...]
        ms = jnp.mean(y * y, axis=-1, keepdims=True)
        out_ref[...] = y * lax.rsqrt(ms + RMS_EPS) * g_ref[...]


def _ffn(h2, x1, wg, wu, wd, g_final, *, tm, tf):
    n, d = x1.shape
    dff = wg.shape[1]
    assert n % tm == 0 and dff % tf == 0
    return pl.pallas_call(
        _ffn_kernel,
        out_shape=jax.ShapeDtypeStruct((n, d), F32),
        grid=(n // tm, dff // tf),
        in_specs=[
            pl.BlockSpec((tm, d), lambda i, j: (i, 0)),
            pl.BlockSpec((tm, d), lambda i, j: (i, 0)),
            pl.BlockSpec((d, tf), lambda i, j: (0, j)),
            pl.BlockSpec((d, tf), lambda i, j: (0, j)),
            pl.BlockSpec((tf, d), lambda i, j: (j, 0)),
            pl.BlockSpec((1, d), lambda i, j: (0, 0)),
        ],
        out_specs=pl.BlockSpec((tm, d), lambda i, j: (i, 0)),
        scratch_shapes=[pltpu.VMEM((tm, d), F32)],
        compiler_params=_cparams(("parallel", "arbitrary")),
        name="swiglu_ffn",
    )(h2, x1, wg, wu, wd, g_final.reshape(1, d))


def _rope_tables(seq):
    half = ROT_DIM // 2
    pos = jnp.arange(seq, dtype=F32)
    inv_freq = ROPE_THETA ** (-jnp.arange(0, ROT_DIM, 2, dtype=F32) / ROT_DIM)
    ang = pos[:, None] * inv_freq[None, :]
    cos, sin = jnp.cos(ang), jnp.sin(ang)
    ones = jnp.ones((seq, HEAD_DIM - ROT_DIM), F32)
    zeros = lambda w: jnp.zeros((seq, w), F32)
    rope_c = jnp.concatenate([cos, cos, ones], axis=1)
    rope_sa = jnp.concatenate([-sin, zeros(HEAD_DIM - half)], axis=1)
    rope_sb = jnp.concatenate([zeros(half), sin, zeros(HEAD_DIM - ROT_DIM)], axis=1)
    return rope_c, rope_sa, rope_sb


def _tile(n, pref):
    t = min(n, pref)
    while n % t:
        t //= 2
    return t


def kernel(x, g_mix, w_in, lambda_q1, lambda_k1, lambda_q2, lambda_k2, g_subln, w_attn_branch,
           w_four_branch, w_out, g_ffn, w_gate, w_up, w_down, g_final):
    b, s, d = x.shape
    n = b * s
    v_width = w_attn_branch.shape[0]
    f_width = w_four_branch.shape[0]
    vd = g_subln.shape[0]
    n_heads = v_width // vd
    qk_width = n_heads * 2 * HEAD_DIM
    f_start = 2 * qk_width + v_width
    gate_start = f_start + f_width
    assert vd == 2 * HEAD_DIM and w_in.shape[1] == gate_start + 2 * d
    assert f_start % f_width == 0 and s % (FFT_RADIX * 2 * V7X_SUBLANES) == 0

    x2 = x.reshape(n, d)
    rope_c, rope_sa, rope_sb = _rope_tables(s)
    z = _in_proj(x2, g_mix, w_in.astype(BF16), rope_c, rope_sa, rope_sb, seq=s, qk_width=qk_width,
                 gate_start=gate_start, tm=_tile(s, 1024), tn=512)

    lamv = jnp.stack([lambda_q1, lambda_k1, lambda_q2, lambda_k2]).astype(F32)
    o = _attention(z.reshape(b, s, -1), lamv, g_subln, n_heads=n_heads, tq=_tile(s, 256))

    cd, twc, tws, cs = _fourier_tables(s)
    fm = _fourier(z.reshape(b, FFT_RADIX, s // FFT_RADIX, -1), cd, twc, tws, cs,
                  f_col_block=f_start // f_width, c=f_width)

    x1, h2 = _merge(o.reshape(n, v_width), fm.reshape(n, f_width), z, x2, w_attn_branch.astype(BF16),
                    w_four_branch.astype(BF16), w_out.astype(BF16), g_ffn, gate_start=gate_start,
                    tm=_tile(n, 512))

    out = _ffn(h2, x1, w_gate.astype(BF16), w_up.astype(BF16), w_down.astype(BF16), g_final,
               tm=_tile(n, 512), tf=512)
    return out.reshape(b, s, d)
```

```python
import functools
import math

import jax
import jax.numpy as jnp
from jax import lax
from jax.experimental import pallas as pl
from jax.experimental.pallas import tpu as pltpu

F32 = jnp.float32
BF16 = jnp.bfloat16

HEAD_DIM = 128
ROT_DIM = HEAD_DIM // 4
ROPE_THETA = 500000.0
LAMBDA_INIT = 0.8 - 0.6 * math.exp(-0.3 * 0)
FOURIER_GROUP_DIM = 128
RMS_EPS = 1e-5
FFT_RADIX = 16

V7X_LANES = 128
V7X_SUBLANES = 8
V7X_VMEM_LIMIT_BYTES = 58 * 1024 * 1024


def _cparams(sem):
    return pltpu.CompilerParams(dimension_semantics=sem, vmem_limit_bytes=V7X_VMEM_LIMIT_BYTES)


def _in_proj_kernel(x_ref, g_ref, w_ref, cos_ref, sa_ref, sb_ref, z_ref, h_ref, *,
                    n_q_tiles, n_rope_tiles, first_gate_tile, q_scale):
    j = pl.program_id(1)

    @pl.when(j == 0)
    def _():
        x = x_ref[...]
        ms = jnp.mean(x * x, axis=-1, keepdims=True)
        h_ref[...] = (x * lax.rsqrt(ms + RMS_EPS) * g_ref[...]).astype(BF16)

    acc = jnp.dot(h_ref[...], w_ref[...], preferred_element_type=F32)
    tn = acc.shape[1]

    @pl.when(j < n_rope_tiles)
    def _():
        rep = tn // HEAD_DIM
        c = jnp.concatenate([cos_ref[...]] * rep, axis=1)
        sa = jnp.concatenate([sa_ref[...]] * rep, axis=1)
        sb = jnp.concatenate([sb_ref[...]] * rep, axis=1)
        a = acc * jnp.where(j < n_q_tiles, q_scale, 1.0).astype(F32)
        half = ROT_DIM // 2
        r = a * c + pltpu.roll(a, tn - half, 1) * sa + pltpu.roll(a, half, 1) * sb
        z_ref[...] = r.astype(BF16)

    @pl.when(jnp.logical_and(j >= n_rope_tiles, j < first_gate_tile))
    def _():
        z_ref[...] = acc.astype(BF16)

    @pl.when(j >= first_gate_tile)
    def _():
        z_ref[...] = jax.nn.sigmoid(acc).astype(BF16)


def _in_proj(x2, g_mix, w_in, rope_c, rope_sa, rope_sb, *, seq, qk_width, gate_start, tm, tn):
    n, d = x2.shape
    in_width = w_in.shape[1]
    assert n % tm == 0 and seq % tm == 0 and in_width % tn == 0
    assert qk_width % tn == 0 and gate_start % tn == 0 and tn % HEAD_DIM == 0
    s_tiles = seq // tm
    kern = functools.partial(
        _in_proj_kernel, n_q_tiles=qk_width // tn, n_rope_tiles=2 * qk_width // tn,
        first_gate_tile=gate_start // tn, q_scale=math.log2(math.e) / math.sqrt(HEAD_DIM))
    return pl.pallas_call(
        kern,
        out_shape=jax.ShapeDtypeStruct((n, in_width), BF16),
        grid=(n // tm, in_width // tn),
        in_specs=[
            pl.BlockSpec((tm, d), lambda i, j: (i, 0)),
            pl.BlockSpec((1, d), lambda i, j: (0, 0)),
            pl.BlockSpec((d, tn), lambda i, j: (0, j)),
            pl.BlockSpec((tm, HEAD_DIM), lambda i, j: (i % s_tiles, 0)),
            pl.BlockSpec((tm, HEAD_DIM), lambda i, j: (i % s_tiles, 0)),
            pl.BlockSpec((tm, HEAD_DIM), lambda i, j: (i % s_tiles, 0)),
        ],
        out_specs=pl.BlockSpec((tm, tn), lambda i, j: (i, j)),
        scratch_shapes=[pltpu.VMEM((tm, d), BF16)],
        compiler_params=_cparams(("parallel", "arbitrary")),
        name="in_proj",
    )(x2, g_mix.reshape(1, d), w_in, rope_c, rope_sa, rope_sb)


_NT = (((1,), (1,)), ((), ()))


def _lane_fold(x, op, ways=4):
    n = x.shape[1] // V7X_LANES
    ways = min(ways, n)
    parts = [x[:, i * V7X_LANES:(i + 1) * V7X_LANES] for i in range(ways)]
    for i in range(ways, n):
        parts[i % ways] = op(parts[i % ways], x[:, i * V7X_LANES:(i + 1) * V7X_LANES])
    while len(parts) > 1:
        parts = [op(parts[2 * i], parts[2 * i + 1]) for i in range(len(parts) // 2)] + parts[len(parts) // 2 * 2:]
    return parts[0]


def _tile_rows(tile, tq):
    if isinstance(tile, int):
        return slice(tile * tq, (tile + 1) * tq)
    return pl.ds(pl.multiple_of(tile * tq, tq), tq)


def _attn_kernel(lamv_ref, g_ref, q_ref, k_ref, v_ref, o_ref, s_ref, p_ref, m_ref, mb_ref, l_ref, r_ref, il_ref,
                 *, tq, n_steps):
    seq = k_ref.shape[1]
    n_tiles = seq // tq
    tk = seq // n_steps
    tr = tq // n_steps
    lv = lamv_ref[...]
    lam = (jnp.exp(jnp.sum(lv[0:1] * lv[1:2], axis=-1, keepdims=True))
           - jnp.exp(jnp.sum(lv[2:3] * lv[3:4], axis=-1, keepdims=True)) + LAMBDA_INIT)
    gain = g_ref[...] * (1.0 - LAMBDA_INIT)

    def qk_piece(tile, slot, j):
        rows = _tile_rows(tile, tq)
        cols = slice(j * tk, (j + 1) * tk)
        for c in range(2):
            qc = q_ref[0, rows, c * HEAD_DIM:(c + 1) * HEAD_DIM]
            kc = k_ref[0, cols, c * HEAD_DIM:(c + 1) * HEAD_DIM]
            s = lax.dot_general(qc, kc, _NT, preferred_element_type=F32)
            s_ref[slot, c, :, cols] = s
            mx = _lane_fold(s, jnp.maximum)
            m_ref[slot, c] = mx if j == 0 else jnp.maximum(m_ref[slot, c], mx)

    def exp_prep(slot):
        for c in range(2):
            m = jnp.max(m_ref[slot, c], axis=-1, keepdims=True)
            mb_ref[c] = jnp.broadcast_to(m, (tq, V7X_LANES))

    def exp_piece(slot, i, pace):
        rs = slice(i * tr, (i + 1) * tr)
        for c in range(2):
            mrow = mb_ref[c, rs, :]
            if pace is not None:
                mrow = mrow + 0.0 * jnp.concatenate([pace] * (tr // V7X_SUBLANES), axis=0)
            m = jnp.concatenate([mrow] * (seq // V7X_LANES), axis=1)
            p = jnp.exp2(s_ref[slot, c, rs, :] - m)
            p_ref[slot, c, rs, :] = p.astype(BF16)
            l_ref[slot, c, rs, :] = _lane_fold(p, jnp.add)

    def pv_prep(slot):
        l1 = jnp.sum(l_ref[slot, 0], axis=-1, keepdims=True)
        l2 = jnp.sum(l_ref[slot, 1], axis=-1, keepdims=True)
        r_ref[...] = jnp.broadcast_to(lam * l1 / l2, (tq, V7X_LANES)).astype(BF16)
        il_ref[...] = jnp.broadcast_to(1.0 / l1, (tq, V7X_LANES))

    def pv_piece(slot, i, acc):
        cols = slice(i * tk, (i + 1) * tk)
        r = jnp.concatenate([r_ref[...]] * (tk // V7X_LANES), axis=1)
        a = p_ref[slot, 0, :, cols] - r * p_ref[slot, 1, :, cols]
        d = jnp.dot(a, v_ref[0, cols, :], preferred_element_type=F32)
        return d if acc is None else acc + d

    def pv_finish(tile, acc):
        o = acc * jnp.concatenate([il_ref[...]] * (acc.shape[1] // V7X_LANES), axis=1)
        ms = jnp.mean(o * o, axis=-1, keepdims=True)
        o_ref[0, _tile_rows(tile, tq), :] = (o * lax.rsqrt(ms + RMS_EPS) * gain).astype(BF16)

    def period(p, par, do_qk, do_exp, do_pv):
        if do_exp:
            exp_prep(1 - par)
        if do_pv:
            pv_prep(par)
        acc = None
        pace = None
        for i in range(n_steps):
            if do_qk:
                qk_piece(p, par, i)
            if do_exp:
                exp_piece(1 - par, i, pace)
            if do_pv:
                acc = pv_piece(par, i, acc)
                pace = acc[0:V7X_SUBLANES, 0:V7X_LANES]
        if do_pv:
            pv_finish(p - 2, acc)

    period(0, 0, True, False, False)
    period(1, 1, True, True, False)

    def body(g, carry):
        period(2 * g + 2, 0, True, True, True)
        period(2 * g + 3, 1, True, True, True)
        return carry

    lax.fori_loop(0, (n_tiles - 2) // 2, body, 0)
    period(n_tiles, 0, False, True, True)
    period(n_tiles + 1, 1, False, False, True)


def _attention(z3, lamv, g_subln, *, n_heads, tq, n_steps):
    b, s, _ = z3.shape
    vd = 2 * HEAD_DIM
    assert s % (2 * tq) == 0 and (s // n_steps) % V7X_LANES == 0 and (tq // n_steps) % 16 == 0
    return pl.pallas_call(
        functools.partial(_attn_kernel, tq=tq, n_steps=n_steps),
        out_shape=jax.ShapeDtypeStruct((b, s, n_heads * vd), BF16),
        grid=(b, n_heads),
        in_specs=[
            pl.BlockSpec((4, HEAD_DIM), lambda bi, h: (0, 0)),
            pl.BlockSpec((1, vd), lambda bi, h: (0, 0)),
            pl.BlockSpec((1, s, vd), lambda bi, h: (bi, 0, h)),
            pl.BlockSpec((1, s, vd), lambda bi, h: (bi, 0, n_heads + h)),
            pl.BlockSpec((1, s, vd), lambda bi, h: (bi, 0, 2 * n_heads + h)),
        ],
        out_specs=pl.BlockSpec((1, s, vd), lambda bi, h: (bi, 0, h)),
        scratch_shapes=[
            pltpu.VMEM((2, 2, tq, s), F32),
            pltpu.VMEM((2, 2, tq, s), BF16),
            pltpu.VMEM((2, 2, tq, V7X_LANES), F32),
            pltpu.VMEM((2, tq, V7X_LANES), F32),
            pltpu.VMEM((2, 2, tq, V7X_LANES), F32),
            pltpu.VMEM((tq, V7X_LANES), BF16),
            pltpu.VMEM((tq, V7X_LANES), F32),
        ],
        compiler_params=_cparams(("parallel", "parallel")),
        name="diff_attn",
    )(lamv, g_subln.reshape(1, vd), z3, z3, z3)


def _cmul_const(x, wr, wi):
    xr, xi = x
    wr = 0.0 if abs(wr) < 1e-12 else wr
    wi = 0.0 if abs(wi) < 1e-12 else wi
    if wi == 0.0:
        return x if wr == 1.0 else ((-xr, -xi) if wr == -1.0 else (xr * wr, xi * wr))
    if wr == 0.0:
        return (xi, -xr) if wi == -1.0 else ((-xi, xr) if wi == 1.0 else (-xi * wi, xr * wi))
    return (xr * wr - xi * wi, xr * wi + xi * wr)


def _fft_list(xs):
    n = len(xs)
    if n == 1:
        return xs
    ev = _fft_list(xs[0::2])
    od = _fft_list(xs[1::2])
    out = [None] * n
    for k in range(n // 2):
        ang = -2.0 * math.pi * k / n
        tr, ti = _cmul_const(od[k], math.cos(ang), math.sin(ang))
        out[k] = (ev[k][0] + tr, ev[k][1] + ti)
        out[k + n // 2] = (ev[k][0] - tr, ev[k][1] - ti)
    return out


def _fourier_kernel(f_ref, cd_ref, twc_ref, tws_ref, cs_ref, out_ref, g_ref):
    _, n1, n2, c = f_ref.shape
    n_groups = c // FOURIER_GROUP_DIM
    f2 = f_ref[0].reshape(n1 * n2, c)
    for grp in range(n_groups):
        lo = grp * FOURIER_GROUP_DIM
        gg = jnp.dot(f2[:, lo:lo + FOURIER_GROUP_DIM], cd_ref[...], preferred_element_type=F32)
        g_ref[:, :, lo:lo + FOURIER_GROUP_DIM] = gg[:, :FOURIER_GROUP_DIM].reshape(n1, n2, FOURIER_GROUP_DIM)
        g_ref[:, :, c + lo:c + lo + FOURIER_GROUP_DIM] = gg[:, FOURIER_GROUP_DIM:].reshape(n1, n2, FOURIER_GROUP_DIM)

    def body(t, carry):
        r = pl.multiple_of(t * V7X_SUBLANES, V7X_SUBLANES)
        rows = pl.ds(r, V7X_SUBLANES)
        for lc in range(c // V7X_LANES):
            re_l = slice(lc * V7X_LANES, (lc + 1) * V7X_LANES)
            im_l = slice(c + lc * V7X_LANES, c + (lc + 1) * V7X_LANES)
            ys = _fft_list([(g_ref[a, rows, re_l], g_ref[a, rows, im_l]) for a in range(n1)])
            for k1 in range(n1):
                yr, yi = ys[k1]
                tc = twc_ref[k1, rows, :]
                ts = tws_ref[k1, rows, :]
                g_ref[k1, rows, re_l] = yr * tc + yi * ts
                g_ref[k1, rows, im_l] = yi * tc - yr * ts
        return carry

    lax.fori_loop(0, n2 // V7X_SUBLANES, body, 0)

    cc = cs_ref[:, :n2]
    ss = cs_ref[:, n2:]
    for k1 in range(n1):
        zr = g_ref[k1, :, :c].astype(BF16)
        zi = g_ref[k1, :, c:].astype(BF16)
        xk = (jnp.dot(cc, zr, preferred_element_type=F32) + jnp.dot(ss, zi, preferred_element_type=F32))
        out_ref[0, :, k1 * c:(k1 + 1) * c] = xk.astype(BF16)


def _fourier(z4, cd, twc, tws, cs, *, f_col_block, c):
    b, n1, n2, _ = z4.shape
    return pl.pallas_call(
        _fourier_kernel,
        out_shape=jax.ShapeDtypeStruct((b, n2, n1 * c), BF16),
        grid=(b,),
        in_specs=[
            pl.BlockSpec((1, n1, n2, c), lambda bi: (bi, 0, 0, f_col_block)),
            pl.BlockSpec(cd.shape, lambda bi: (0, 0)),
            pl.BlockSpec(twc.shape, lambda bi: (0, 0, 0)),
            pl.BlockSpec(tws.shape, lambda bi: (0, 0, 0)),
            pl.BlockSpec(cs.shape, lambda bi: (0, 0)),
        ],
        out_specs=pl.BlockSpec((1, n2, n1 * c), lambda bi: (bi, 0, 0)),
        scratch_shapes=[pltpu.VMEM((n1, n2, 2 * c), F32)],
        compiler_params=_cparams(("parallel",)),
        name="fourier_mix",
    )(z4, cd, twc, tws, cs)


def _fourier_tables(seq):
    n1, n2 = FFT_RADIX, seq // FFT_RADIX
    gd = FOURIER_GROUP_DIM

    def cos_sin(n, scale):
        idx = jnp.arange(n, dtype=jnp.int32)
        ang = ((idx[:, None] * idx[None, :]) % n).astype(F32) * (2.0 * math.pi / n)
        return jnp.cos(ang) * scale, jnp.sin(ang) * scale

    cg, sg = cos_sin(gd, gd ** -0.5)
    cd = jnp.concatenate([cg, -sg], axis=1).astype(BF16)
    c2, s2 = cos_sin(n2, seq ** -0.5)
    cs = jnp.concatenate([c2, s2], axis=1).astype(BF16)
    k1 = jnp.arange(n1, dtype=jnp.int32)[:, None]
    r = jnp.arange(n2, dtype=jnp.int32)[None, :]
    ang = (k1 * r).astype(F32) * (2.0 * math.pi / seq)
    twc = jnp.broadcast_to(jnp.cos(ang)[:, :, None], (n1, n2, V7X_LANES))
    tws = jnp.broadcast_to(jnp.sin(ang)[:, :, None], (n1, n2, V7X_LANES))
    return cd, twc, tws, cs


def _merge_kernel(o_ref, fm_ref, ga0_ref, ga1_ref, gf0_ref, gf1_ref, x_ref, wa_ref, wf_ref, wo_ref,
                  g_ref, x1_ref, h2_ref):
    o = o_ref[...]
    fm = fm_ref[...]
    acc = x_ref[...]
    half = ga0_ref.shape[1]
    for ci, (ga_ref, gf_ref) in enumerate(((ga0_ref, gf0_ref), (ga1_ref, gf1_ref))):
        cols = slice(ci * half, (ci + 1) * half)
        br_a = jnp.dot(o, wa_ref[:, cols], preferred_element_type=F32)
        br_f = jnp.dot(fm, wf_ref[:, cols], preferred_element_type=F32)
        m = ga_ref[...].astype(F32) * br_a + gf_ref[...].astype(F32) * br_f
        acc = acc + jnp.dot(m.astype(BF16), wo_ref[cols, :], preferred_element_type=F32)
    x1_ref[...] = acc
    ms = jnp.mean(acc * acc, axis=-1, keepdims=True)
    h2_ref[...] = (acc * lax.rsqrt(ms + RMS_EPS) * g_ref[...]).astype(BF16)


def _merge(o2, fm2, z2, x2, wa, wf, wo, g_ffn, *, gate_start, tm):
    n, d = x2.shape
    half = d // 2
    assert gate_start % half == 0 and n % tm == 0
    gb = gate_start // half
    const = lambda i: (0, 0)
    return pl.pallas_call(
        _merge_kernel,
        out_shape=(jax.ShapeDtypeStruct((n, d), F32), jax.ShapeDtypeStruct((n, d), BF16)),
        grid=(n // tm,),
        in_specs=[
            pl.BlockSpec((tm, o2.shape[1]), lambda i: (i, 0)),
            pl.BlockSpec((tm, fm2.shape[1]), lambda i: (i, 0)),
            pl.BlockSpec((tm, half), lambda i: (i, gb)),
            pl.BlockSpec((tm, half), lambda i: (i, gb + 1)),
            pl.BlockSpec((tm, half), lambda i: (i, gb + 2)),
            pl.BlockSpec((tm, half), lambda i: (i, gb + 3)),
            pl.BlockSpec((tm, d), lambda i: (i, 0)),
            pl.BlockSpec(wa.shape, const, pipeline_mode=pl.Buffered(1)),
            pl.BlockSpec(wf.shape, const, pipeline_mode=pl.Buffered(1)),
            pl.BlockSpec(wo.shape, const, pipeline_mode=pl.Buffered(1)),
            pl.BlockSpec((1, d), const),
        ],
        out_specs=(pl.BlockSpec((tm, d), lambda i: (i, 0)), pl.BlockSpec((tm, d), lambda i: (i, 0))),
        compiler_params=_cparams(("parallel",)),
        name="merge_out_proj",
    )(o2, fm2, z2, z2, z2, z2, x2, wa, wf, wo, g_ffn.reshape(1, d))


def _ffn_kernel(h_ref, x1_ref, wg_ref, wu_ref, wd_ref, g_ref, out_ref, acc_ref):
    j = pl.program_id(1)
    h = h_ref[...]
    gate = jnp.dot(h, wg_ref[...], preferred_element_type=F32)
    up = jnp.dot(h, wu_ref[...], preferred_element_type=F32)
    act = (gate * jax.nn.sigmoid(gate) * up).astype(BF16)
    part = jnp.dot(act, wd_ref[...], preferred_element_type=F32)

    @pl.when(j == 0)
    def _():
        acc_ref[...] = x1_ref[...] + part

    @pl.when(j > 0)
    def _():
        acc_ref[...] += part

    @pl.when(j == pl.num_programs(1) - 1)
    def _():
        y = acc_ref[...]
        ms = jnp.mean(y * y, axis=-1, keepdims=True)
        out_ref[...] = y * lax.rsqrt(ms + RMS_EPS) * g_ref[...]


def _ffn(h2, x1, wg, wu, wd, g_final, *, tm, tf):
    n, d = x1.shape
    dff = wg.shape[1]
    assert n % tm == 0 and dff % tf == 0
    return pl.pallas_call(
        _ffn_kernel,
        out_shape=jax.ShapeDtypeStruct((n, d), F32),
        grid=(n // tm, dff // tf),
        in_specs=[
            pl.BlockSpec((tm, d), lambda i, j: (i, 0)),
            pl.BlockSpec((tm, d), lambda i, j: (i, 0)),
            pl.BlockSpec((d, tf), lambda i, j: (0, j)),
            pl.BlockSpec((d, tf), lambda i, j: (0, j)),
            pl.BlockSpec((tf, d), lambda i, j: (j, 0)),
            pl.BlockSpec((1, d), lambda i, j: (0, 0)),
        ],
        out_specs=pl.BlockSpec((tm, d), lambda i, j: (i, 0)),
        scratch_shapes=[pltpu.VMEM((tm, d), F32)],
        compiler_params=_cparams(("parallel", "arbitrary")),
        name="swiglu_ffn",
    )(h2, x1, wg, wu, wd, g_final.reshape(1, d))


def _rope_tables(seq):
    half = ROT_DIM // 2
    pos = jnp.arange(seq, dtype=F32)
    inv_freq = ROPE_THETA ** (-jnp.arange(0, ROT_DIM, 2, dtype=F32) / ROT_DIM)
    ang = pos[:, None] * inv_freq[None, :]
    cos, sin = jnp.cos(ang), jnp.sin(ang)
    ones = jnp.ones((seq, HEAD_DIM - ROT_DIM), F32)
    zeros = lambda w: jnp.zeros((seq, w), F32)
    rope_c = jnp.concatenate([cos, cos, ones], axis=1)
    rope_sa = jnp.concatenate([-sin, zeros(HEAD_DIM - half)], axis=1)
    rope_sb = jnp.concatenate([zeros(half), sin, zeros(HEAD_DIM - ROT_DIM)], axis=1)
    return rope_c, rope_sa, rope_sb


def _tile(n, pref):
    t = min(n, pref)
    while n % t:
        t //= 2
    return t


def kernel(x, g_mix, w_in, lambda_q1, lambda_k1, lambda_q2, lambda_k2, g_subln, w_attn_branch,
           w_four_branch, w_out, g_ffn, w_gate, w_up, w_down, g_final):
    b, s, d = x.shape
    n = b * s
    v_width = w_attn_branch.shape[0]
    f_width = w_four_branch.shape[0]
    vd = g_subln.shape[0]
    n_heads = v_width // vd
    qk_width = n_heads * 2 * HEAD_DIM
    f_start = 2 * qk_width + v_width
    gate_start = f_start + f_width
    assert vd == 2 * HEAD_DIM and w_in.shape[1] == gate_start + 2 * d
    assert f_start % f_width == 0 and s % (FFT_RADIX * 2 * V7X_SUBLANES) == 0

    x2 = x.reshape(n, d)
    rope_c, rope_sa, rope_sb = _rope_tables(s)
    z = _in_proj(x2, g_mix, w_in.astype(BF16), rope_c, rope_sa, rope_sb, seq=s, qk_width=qk_width,
                 gate_start=gate_start, tm=_tile(s, 1024), tn=512)

    lamv = jnp.stack([lambda_q1, lambda_k1, lambda_q2, lambda_k2]).astype(F32)
    o = _attention(z.reshape(b, s, -1), lamv, g_subln, n_heads=n_heads, tq=256, n_steps=min(16, s // 256))

    cd, twc, tws, cs = _fourier_tables(s)
    fm = _fourier(z.reshape(b, FFT_RADIX, s // FFT_RADIX, -1), cd, twc, tws, cs,
                  f_col_block=f_start // f_width, c=f_width)

    x1, h2 = _merge(o.reshape(n, v_width), fm.reshape(n, f_width), z, x2, w_attn_branch.astype(BF16),
                    w_four_branch.astype(BF16), w_out.astype(BF16), g_ffn, gate_start=gate_start,
                    tm=_tile(n, 512))

    out = _ffn(h2, x1, w_gate.astype(BF16), w_up.astype(BF16), w_down.astype(BF16), g_final,
               tm=_tile(n, 512), tf=512)
    return out.reshape(b, s, d)
```

```python
import functools
import math

import jax
import jax.numpy as jnp
from jax import lax
from jax.experimental import pallas as pl
from jax.experimental.pallas import tpu as pltpu

F32 = jnp.float32
BF16 = jnp.bfloat16

HEAD_DIM = 128
ROT_DIM = HEAD_DIM // 4
ROPE_THETA = 500000.0
LAMBDA_INIT = 0.8 - 0.6 * math.exp(-0.3 * 0)
FOURIER_GROUP_DIM = 128
RMS_EPS = 1e-5
FFT_RADIX = 16

V7X_LANES = 128
V7X_SUBLANES = 8
V7X_VMEM_LIMIT_BYTES = 58 * 1024 * 1024


def _cparams(sem):
    return pltpu.CompilerParams(dimension_semantics=sem, vmem_limit_bytes=V7X_VMEM_LIMIT_BYTES)


def _rms_cast_kernel(x_ref, g_ref, h_ref):
    x = x_ref[...]
    ms = jnp.mean(x * x, axis=-1, keepdims=True)
    h_ref[...] = (x * lax.rsqrt(ms + RMS_EPS) * g_ref[...]).astype(BF16)


def _rms_cast(x2, g, *, tm):
    n, d = x2.shape
    return pl.pallas_call(
        _rms_cast_kernel,
        out_shape=jax.ShapeDtypeStruct((n, d), BF16),
        grid=(n // tm,),
        in_specs=[pl.BlockSpec((tm, d), lambda i: (i, 0)), pl.BlockSpec((1, d), lambda i: (0, 0))],
        out_specs=pl.BlockSpec((tm, d), lambda i: (i, 0)),
        compiler_params=_cparams(("parallel",)),
        name="rms_cast",
    )(x2, g.reshape(1, d))


def _proj_rope_kernel(h_ref, w_ref, cos_ref, sa_ref, sb_ref, z_ref):
    acc = jnp.dot(h_ref[...], w_ref[...], preferred_element_type=F32)
    tn = acc.shape[1]
    rep = tn // HEAD_DIM
    c = jnp.concatenate([cos_ref[0]] * rep, axis=1)
    sa = jnp.concatenate([sa_ref[0]] * rep, axis=1)
    sb = jnp.concatenate([sb_ref[0]] * rep, axis=1)
    half = ROT_DIM // 2
    r = acc * c + pltpu.roll(acc, tn - half, 1) * sa + pltpu.roll(acc, half, 1) * sb
    z_ref[...] = r.astype(BF16)


def _proj_plain_kernel(h_ref, w_ref, z_ref):
    z_ref[...] = jnp.dot(h_ref[...], w_ref[...], preferred_element_type=F32).astype(BF16)


def _proj_sigmoid_kernel(h_ref, w_ref, z_ref):
    z_ref[...] = jax.nn.sigmoid(jnp.dot(h_ref[...], w_ref[...], preferred_element_type=F32)).astype(BF16)


def _proj(h, w, *, tm, tn, name, rope=None, seq=None, q_width=None, sigmoid=False):
    n, d = h.shape
    width = w.shape[1]
    assert n % tm == 0 and width % tn == 0
    in_specs = [pl.BlockSpec((tm, d), lambda i, j: (i, 0)), pl.BlockSpec((d, tn), lambda i, j: (0, j))]
    args = [h, w]
    if rope is not None:
        assert seq % tm == 0 and q_width % tn == 0 and tn % HEAD_DIM == 0
        s_tiles, q_tiles = seq // tm, q_width // tn
        in_specs += [pl.BlockSpec((1, tm, HEAD_DIM), lambda i, j: (j // q_tiles, i % s_tiles, 0))] * 3
        args += list(rope)
        kern = _proj_rope_kernel
    else:
        kern = _proj_sigmoid_kernel if sigmoid else _proj_plain_kernel
    return pl.pallas_call(
        kern,
        out_shape=jax.ShapeDtypeStruct((n, width), BF16),
        grid=(n // tm, width // tn),
        in_specs=in_specs,
        out_specs=pl.BlockSpec((tm, tn), lambda i, j: (i, j)),
        compiler_params=_cparams(("parallel", "arbitrary")),
        name=name,
    )(*args)


_NT = (((1,), (1,)), ((), ()))


def _lane_fold(x, op, ways=4):
    n = x.shape[1] // V7X_LANES
    ways = min(ways, n)
    parts = [x[:, i * V7X_LANES:(i + 1) * V7X_LANES] for i in range(ways)]
    for i in range(ways, n):
        parts[i % ways] = op(parts[i % ways], x[:, i * V7X_LANES:(i + 1) * V7X_LANES])
    while len(parts) > 1:
        parts = [op(parts[2 * i], parts[2 * i + 1]) for i in range(len(parts) // 2)] + parts[len(parts) // 2 * 2:]
    return parts[0]


def _tile_rows(tile, tq):
    if isinstance(tile, int):
        return slice(tile * tq, (tile + 1) * tq)
    return pl.ds(pl.multiple_of(tile * tq, tq), tq)


def _attn_kernel(lamv_ref, g_ref, q_ref, k_ref, v_ref, o_ref, s_ref, p_ref, m_ref, mb_ref, l_ref, r_ref, il_ref,
                 *, tq, n_steps):
    seq = k_ref.shape[1]
    n_tiles = seq // tq
    tk = seq // n_steps
    tr = tq // n_steps
    lv = lamv_ref[...]
    lam = (jnp.exp(jnp.sum(lv[0:1] * lv[1:2], axis=-1, keepdims=True))
           - jnp.exp(jnp.sum(lv[2:3] * lv[3:4], axis=-1, keepdims=True)) + LAMBDA_INIT)
    gain = g_ref[...] * (1.0 - LAMBDA_INIT)

    def qk_piece(tile, slot, j):
        rows = _tile_rows(tile, tq)
        cols = slice(j * tk, (j + 1) * tk)
        for c in range(2):
            qc = q_ref[0, rows, c * HEAD_DIM:(c + 1) * HEAD_DIM]
            kc = k_ref[0, cols, c * HEAD_DIM:(c + 1) * HEAD_DIM]
            s = lax.dot_general(qc, kc, _NT, preferred_element_type=F32)
            s_ref[slot, c, :, cols] = s
            mx = _lane_fold(s, jnp.maximum)
            m_ref[slot, c] = mx if j == 0 else jnp.maximum(m_ref[slot, c], mx)

    def exp_prep(slot):
        for c in range(2):
            m = jnp.max(m_ref[slot, c], axis=-1, keepdims=True)
            mb_ref[c] = jnp.broadcast_to(m, (tq, V7X_LANES))

    def exp_piece(slot, i, pace):
        rs = slice(i * tr, (i + 1) * tr)
        for c in range(2):
            mrow = mb_ref[c, rs, :]
            if pace is not None:
                mrow = mrow + 0.0 * jnp.concatenate([pace] * (tr // V7X_SUBLANES), axis=0)
            m = jnp.concatenate([mrow] * (seq // V7X_LANES), axis=1)
            p = jnp.exp2(s_ref[slot, c, rs, :] - m)
            p_ref[slot, c, rs, :] = p.astype(BF16)
            l_ref[slot, c, rs, :] = _lane_fold(p, jnp.add)

    def pv_prep(slot):
        l1 = jnp.sum(l_ref[slot, 0], axis=-1, keepdims=True)
        l2 = jnp.sum(l_ref[slot, 1], axis=-1, keepdims=True)
        r_ref[...] = jnp.broadcast_to(lam * l1 / l2, (tq, V7X_LANES)).astype(BF16)
        il_ref[...] = jnp.broadcast_to(1.0 / l1, (tq, V7X_LANES))

    def pv_piece(slot, i, acc):
        cols = slice(i * tk, (i + 1) * tk)
        r = jnp.concatenate([r_ref[...]] * (tk // V7X_LANES), axis=1)
        a = p_ref[slot, 0, :, cols] - r * p_ref[slot, 1, :, cols]
        d = jnp.dot(a, v_ref[0, cols, :], preferred_element_type=F32)
        return d if acc is None else acc + d

    def pv_finish(tile, acc):
        o = acc * jnp.concatenate([il_ref[...]] * (acc.shape[1] // V7X_LANES), axis=1)
        ms = jnp.mean(o * o, axis=-1, keepdims=True)
        o_ref[0, _tile_rows(tile, tq), :] = (o * lax.rsqrt(ms + RMS_EPS) * gain).astype(BF16)

    def period(p, par, do_qk, do_exp, do_pv):
        if do_exp:
            exp_prep(1 - par)
        if do_pv:
            pv_prep(par)
        acc = None
        pace = None
        for i in range(n_steps):
            if do_qk:
                qk_piece(p, par, i)
            if do_exp:
                exp_piece(1 - par, i, pace)
            if do_pv:
                acc = pv_piece(par, i, acc)
                pace = acc[0:V7X_SUBLANES, 0:V7X_LANES]
        if do_pv:
            pv_finish(p - 2, acc)

    period(0, 0, True, False, False)
    period(1, 1, True, True, False)

    def body(g, carry):
        period(2 * g + 2, 0, True, True, True)
        period(2 * g + 3, 1, True, True, True)
        return carry

    lax.fori_loop(0, (n_tiles - 2) // 2, body, 0)
    period(n_tiles, 0, False, True, True)
    period(n_tiles + 1, 1, False, False, True)


def _attention(qk3, vf3, lamv, g_subln, *, n_heads, tq, n_steps):
    b, s, _ = qk3.shape
    vd = 2 * HEAD_DIM
    assert s % (2 * tq) == 0 and (s // n_steps) % V7X_LANES == 0 and (tq // n_steps) % 16 == 0
    return pl.pallas_call(
        functools.partial(_attn_kernel, tq=tq, n_steps=n_steps),
        out_shape=jax.ShapeDtypeStruct((b, s, n_heads * vd), BF16),
        grid=(b, n_heads),
        in_specs=[
            pl.BlockSpec((4, HEAD_DIM), lambda bi, h: (0, 0)),
            pl.BlockSpec((1, vd), lambda bi, h: (0, 0)),
            pl.BlockSpec((1, s, vd), lambda bi, h: (bi, 0, h)),
            pl.BlockSpec((1, s, vd), lambda bi, h: (bi, 0, n_heads + h)),
            pl.BlockSpec((1, s, vd), lambda bi, h: (bi, 0, h)),
        ],
        out_specs=pl.BlockSpec((1, s, vd), lambda bi, h: (bi, 0, h)),
        scratch_shapes=[
            pltpu.VMEM((2, 2, tq, s), F32),
            pltpu.VMEM((2, 2, tq, s), BF16),
            pltpu.VMEM((2, 2, tq, V7X_LANES), F32),
            pltpu.VMEM((2, tq, V7X_LANES), F32),
            pltpu.VMEM((2, 2, tq, V7X_LANES), F32),
            pltpu.VMEM((tq, V7X_LANES), BF16),
            pltpu.VMEM((tq, V7X_LANES), F32),
        ],
        compiler_params=_cparams(("parallel", "parallel")),
        name="diff_attn",
    )(lamv, g_subln.reshape(1, vd), qk3, qk3, vf3)


def _cmul_const(x, wr, wi):
    xr, xi = x
    wr = 0.0 if abs(wr) < 1e-12 else wr
    wi = 0.0 if abs(wi) < 1e-12 else wi
    if wi == 0.0:
        return x if wr == 1.0 else ((-xr, -xi) if wr == -1.0 else (xr * wr, xi * wr))
    if wr == 0.0:
        return (xi, -xr) if wi == -1.0 else ((-xi, xr) if wi == 1.0 else (-xi * wi, xr * wi))
    return (xr * wr - xi * wi, xr * wi + xi * wr)


def _fft_list(xs):
    n = len(xs)
    if n == 1:
        return xs
    ev = _fft_list(xs[0::2])
    od = _fft_list(xs[1::2])
    out = [None] * n
    for k in range(n // 2):
        ang = -2.0 * math.pi * k / n
        tr, ti = _cmul_const(od[k], math.cos(ang), math.sin(ang))
        out[k] = (ev[k][0] + tr, ev[k][1] + ti)
        out[k + n // 2] = (ev[k][0] - tr, ev[k][1] - ti)
    return out


def _fourier_kernel(f_ref, cd_ref, twc_ref, tws_ref, cs_ref, out_ref, g_ref):
    _, n1, n2, c = f_ref.shape
    n_groups = c // FOURIER_GROUP_DIM
    f2 = f_ref[0].reshape(n1 * n2, c)
    for grp in range(n_groups):
        lo = grp * FOURIER_GROUP_DIM
        gg = jnp.dot(f2[:, lo:lo + FOURIER_GROUP_DIM], cd_ref[...], preferred_element_type=F32)
        g_ref[:, :, lo:lo + FOURIER_GROUP_DIM] = gg[:, :FOURIER_GROUP_DIM].reshape(n1, n2, FOURIER_GROUP_DIM)
        g_ref[:, :, c + lo:c + lo + FOURIER_GROUP_DIM] = gg[:, FOURIER_GROUP_DIM:].reshape(n1, n2, FOURIER_GROUP_DIM)

    def body(t, carry):
        r = pl.multiple_of(t * V7X_SUBLANES, V7X_SUBLANES)
        rows = pl.ds(r, V7X_SUBLANES)
        for lc in range(c // V7X_LANES):
            re_l = slice(lc * V7X_LANES, (lc + 1) * V7X_LANES)
            im_l = slice(c + lc * V7X_LANES, c + (lc + 1) * V7X_LANES)
            ys = _fft_list([(g_ref[a, rows, re_l], g_ref[a, rows, im_l]) for a in range(n1)])
            for k1 in range(n1):
                yr, yi = ys[k1]
                tc = twc_ref[k1, rows, :]
                ts = tws_ref[k1, rows, :]
                g_ref[k1, rows, re_l] = yr * tc + yi * ts
                g_ref[k1, rows, im_l] = yi * tc - yr * ts
        return carry

    lax.fori_loop(0, n2 // V7X_SUBLANES, body, 0)

    cc = cs_ref[:, :n2]
    ss = cs_ref[:, n2:]
    for k1 in range(n1):
        zr = g_ref[k1, :, :c].astype(BF16)
        zi = g_ref[k1, :, c:].astype(BF16)
        xk = (jnp.dot(cc, zr, preferred_element_type=F32) + jnp.dot(ss, zi, preferred_element_type=F32))
        out_ref[0, :, k1 * c:(k1 + 1) * c] = xk.astype(BF16)


def _fourier(z4, cd, twc, tws, cs, *, f_col_block, c):
    b, n1, n2, _ = z4.shape
    return pl.pallas_call(
        _fourier_kernel,
        out_shape=jax.ShapeDtypeStruct((b, n2, n1 * c), BF16),
        grid=(b,),
        in_specs=[
            pl.BlockSpec((1, n1, n2, c), lambda bi: (bi, 0, 0, f_col_block)),
            pl.BlockSpec(cd.shape, lambda bi: (0, 0)),
            pl.BlockSpec(twc.shape, lambda bi: (0, 0, 0)),
            pl.BlockSpec(tws.shape, lambda bi: (0, 0, 0)),
            pl.BlockSpec(cs.shape, lambda bi: (0, 0)),
        ],
        out_specs=pl.BlockSpec((1, n2, n1 * c), lambda bi: (bi, 0, 0)),
        scratch_shapes=[pltpu.VMEM((n1, n2, 2 * c), F32)],
        compiler_params=_cparams(("parallel",)),
        name="fourier_mix",
    )(z4, cd, twc, tws, cs)


def _fourier_tables(seq):
    n1, n2 = FFT_RADIX, seq // FFT_RADIX
    gd = FOURIER_GROUP_DIM

    def cos_sin(n, scale):
        idx = jnp.arange(n, dtype=jnp.int32)
        ang = ((idx[:, None] * idx[None, :]) % n).astype(F32) * (2.0 * math.pi / n)
        return jnp.cos(ang) * scale, jnp.sin(ang) * scale

    cg, sg = cos_sin(gd, gd ** -0.5)
    cd = jnp.concatenate([cg, -sg], axis=1).astype(BF16)
    c2, s2 = cos_sin(n2, seq ** -0.5)
    cs = jnp.concatenate([c2, s2], axis=1).astype(BF16)
    k1 = jnp.arange(n1, dtype=jnp.int32)[:, None]
    r = jnp.arange(n2, dtype=jnp.int32)[None, :]
    ang = (k1 * r).astype(F32) * (2.0 * math.pi / seq)
    twc = jnp.broadcast_to(jnp.cos(ang)[:, :, None], (n1, n2, V7X_LANES))
    tws = jnp.broadcast_to(jnp.sin(ang)[:, :, None], (n1, n2, V7X_LANES))
    return cd, twc, tws, cs


def _merge_kernel(o_ref, fm_ref, ga0_ref, ga1_ref, gf0_ref, gf1_ref, x_ref, wa_ref, wf_ref, wo_ref,
                  g_ref, x1_ref, h2_ref):
    o = o_ref[...]
    fm = fm_ref[...]
    acc = x_ref[...]
    half = ga0_ref.shape[1]
    for ci, (ga_ref, gf_ref) in enumerate(((ga0_ref, gf0_ref), (ga1_ref, gf1_ref))):
        cols = slice(ci * half, (ci + 1) * half)
        br_a = jnp.dot(o, wa_ref[:, cols], preferred_element_type=F32)
        br_f = jnp.dot(fm, wf_ref[:, cols], preferred_element_type=F32)
        m = ga_ref[...].astype(F32) * br_a + gf_ref[...].astype(F32) * br_f
        acc = acc + jnp.dot(m.astype(BF16), wo_ref[cols, :], preferred_element_type=F32)
    x1_ref[...] = acc
    ms = jnp.mean(acc * acc, axis=-1, keepdims=True)
    h2_ref[...] = (acc * lax.rsqrt(ms + RMS_EPS) * g_ref[...]).astype(BF16)


def _merge(o2, fm2, gates, x2, wa, wf, wo, g_ffn, *, tm):
    n, d = x2.shape
    half = d // 2
    assert gates.shape[1] == 2 * d and n % tm == 0
    const = lambda i: (0, 0)
    return pl.pallas_call(
        _merge_kernel,
        out_shape=(jax.ShapeDtypeStruct((n, d), F32), jax.ShapeDtypeStruct((n, d), BF16)),
        grid=(n // tm,),
        in_specs=[
            pl.BlockSpec((tm, o2.shape[1]), lambda i: (i, 0)),
            pl.BlockSpec((tm, fm2.shape[1]), lambda i: (i, 0)),
            pl.BlockSpec((tm, half), lambda i: (i, 0)),
            pl.BlockSpec((tm, half), lambda i: (i, 1)),
            pl.BlockSpec((tm, half), lambda i: (i, 2)),
            pl.BlockSpec((tm, half), lambda i: (i, 3)),
            pl.BlockSpec((tm, d), lambda i: (i, 0)),
            pl.BlockSpec(wa.shape, const, pipeline_mode=pl.Buffered(1)),
            pl.BlockSpec(wf.shape, const, pipeline_mode=pl.Buffered(1)),
            pl.BlockSpec(wo.shape, const, pipeline_mode=pl.Buffered(1)),
            pl.BlockSpec((1, d), const),
        ],
        out_specs=(pl.BlockSpec((tm, d), lambda i: (i, 0)), pl.BlockSpec((tm, d), lambda i: (i, 0))),
        compiler_params=_cparams(("parallel",)),
        name="merge_out_proj",
    )(o2, fm2, gates, gates, gates, gates, x2, wa, wf, wo, g_ffn.reshape(1, d))


def _ffn_kernel(h_ref, x1_ref, wg_ref, wu_ref, wd_ref, g_ref, out_ref, acc_ref):
    j = pl.program_id(1)

    @pl.when(j == 0)
    def _():
        acc_ref[...] = x1_ref[...]

    h = h_ref[...]
    gate = jnp.dot(h, wg_ref[...], preferred_element_type=F32)
    up = jnp.dot(h, wu_ref[...], preferred_element_type=F32)
    act = (gate * jax.nn.sigmoid(gate) * up).astype(BF16)
    acc_ref[...] += jnp.dot(act, wd_ref[...], preferred_element_type=F32)

    @pl.when(j == pl.num_programs(1) - 1)
    def _():
        y = acc_ref[...]
        ms = jnp.mean(y * y, axis=-1, keepdims=True)
        out_ref[...] = y * lax.rsqrt(ms + RMS_EPS) * g_ref[...]


def _ffn(h2, x1, wg, wu, wd, g_final, *, tm, tf):
    n, d = x1.shape
    dff = wg.shape[1]
    assert n % tm == 0 and dff % tf == 0
    return pl.pallas_call(
        _ffn_kernel,
        out_shape=jax.ShapeDtypeStruct((n, d), F32),
        grid=(n // tm, dff // tf),
        in_specs=[
            pl.BlockSpec((tm, d), lambda i, j: (i, 0)),
            pl.BlockSpec((tm, d), lambda i, j: (i, 0)),
            pl.BlockSpec((d, tf), lambda i, j: (0, j)),
            pl.BlockSpec((d, tf), lambda i, j: (0, j)),
            pl.BlockSpec((tf, d), lambda i, j: (j, 0)),
            pl.BlockSpec((1, d), lambda i, j: (0, 0)),
        ],
        out_specs=pl.BlockSpec((tm, d), lambda i, j: (i, 0)),
        scratch_shapes=[pltpu.VMEM((tm, d), F32)],
        compiler_params=_cparams(("parallel", "arbitrary")),
        name="swiglu_ffn",
    )(h2, x1, wg, wu, wd, g_final.reshape(1, d))


def _rope_tables(seq):
    half = ROT_DIM // 2
    pos = jnp.arange(seq, dtype=F32)
    inv_freq = ROPE_THETA ** (-jnp.arange(0, ROT_DIM, 2, dtype=F32) / ROT_DIM)
    ang = pos[:, None] * inv_freq[None, :]
    cos, sin = jnp.cos(ang), jnp.sin(ang)
    ones = jnp.ones((seq, HEAD_DIM - ROT_DIM), F32)
    zeros = lambda w: jnp.zeros((seq, w), F32)
    rope_c = jnp.concatenate([cos, cos, ones], axis=1)
    rope_sa = jnp.concatenate([-sin, zeros(HEAD_DIM - half)], axis=1)
    rope_sb = jnp.concatenate([zeros(half), sin, zeros(HEAD_DIM - ROT_DIM)], axis=1)
    q_scale = math.log2(math.e) / math.sqrt(HEAD_DIM)
    return tuple(jnp.stack([t * q_scale, t]) for t in (rope_c, rope_sa, rope_sb))


def _tile(n, pref):
    t = min(n, pref)
    while n % t:
        t //= 2
    return t


def kernel(x, g_mix, w_in, lambda_q1, lambda_k1, lambda_q2, lambda_k2, g_subln, w_attn_branch,
           w_four_branch, w_out, g_ffn, w_gate, w_up, w_down, g_final):
    b, s, d = x.shape
    n = b * s
    v_width = w_attn_branch.shape[0]
    f_width = w_four_branch.shape[0]
    vd = g_subln.shape[0]
    n_heads = v_width // vd
    qk_width = n_heads * 2 * HEAD_DIM
    f_start = 2 * qk_width + v_width
    gate_start = f_start + f_width
    assert vd == 2 * HEAD_DIM and w_in.shape[1] == gate_start + 2 * d
    assert v_width % f_width == 0 and s % (FFT_RADIX * 2 * V7X_SUBLANES) == 0

    x2 = x.reshape(n, d)
    tm = _tile(s, 1024)
    h = _rms_cast(x2, g_mix, tm=tm)
    w_in_bf = w_in.astype(BF16)
    qk = _proj(h, w_in_bf[:, :2 * qk_width], tm=tm, tn=qk_width, name="proj_qk_rope",
               rope=_rope_tables(s), seq=s, q_width=qk_width)
    vf = _proj(h, w_in_bf[:, 2 * qk_width:gate_start], tm=tm, tn=1024, name="proj_vf")
    gates = _proj(h, w_in_bf[:, gate_start:], tm=tm, tn=1024, name="proj_gates", sigmoid=True)

    lamv = jnp.stack([lambda_q1, lambda_k1, lambda_q2, lambda_k2]).astype(F32)
    o = _attention(qk.reshape(b, s, -1), vf.reshape(b, s, -1), lamv, g_subln, n_heads=n_heads, tq=256,
                   n_steps=min(16, s // 256))

    cd, twc, tws, cs = _fourier_tables(s)
    fm = _fourier(vf.reshape(b, FFT_RADIX, s // FFT_RADIX, -1), cd, twc, tws, cs,
                  f_col_block=v_width // f_width, c=f_width)

    x1, h2 = _merge(o.reshape(n, v_width), fm.reshape(n, f_width), gates, x2, w_attn_branch.astype(BF16),
                    w_four_branch.astype(BF16), w_out.astype(BF16), g_ffn, tm=_tile(n, 512))

    out = _ffn(h2, x1, w_gate.astype(BF16), w_up.astype(BF16), w_down.astype(BF16), g_final,
               tm=_tile(n, 512), tf=512)
    return out.reshape(b, s, d)
```

```python
import functools
import math

import jax
import jax.numpy as jnp
from jax import lax
from jax.experimental import pallas as pl
from jax.experimental.pallas import tpu as pltpu

F32 = jnp.float32
BF16 = jnp.bfloat16

HEAD_DIM = 128
ROT_DIM = HEAD_DIM // 4
ROPE_THETA = 500000.0
LAMBDA_INIT = 0.8 - 0.6 * math.exp(-0.3 * 0)
FOURIER_GROUP_DIM = 128
RMS_EPS = 1e-5
FFT_RADIX = 16

V7X_LANES = 128
V7X_SUBLANES = 8
V7X_VMEM_LIMIT_BYTES = 58 * 1024 * 1024


def _cparams(sem):
    return pltpu.CompilerParams(dimension_semantics=sem, vmem_limit_bytes=V7X_VMEM_LIMIT_BYTES)


def _rms_cast_kernel(x_ref, g_ref, h_ref):
    x = x_ref[...]
    ms = jnp.mean(x * x, axis=-1, keepdims=True)
    h_ref[...] = (x * lax.rsqrt(ms + RMS_EPS) * g_ref[...]).astype(BF16)


def _rms_cast(x2, g, *, tm):
    n, d = x2.shape
    return pl.pallas_call(
        _rms_cast_kernel,
        out_shape=jax.ShapeDtypeStruct((n, d), BF16),
        grid=(n // tm,),
        in_specs=[pl.BlockSpec((tm, d), lambda i: (i, 0)), pl.BlockSpec((1, d), lambda i: (0, 0))],
        out_specs=pl.BlockSpec((tm, d), lambda i: (i, 0)),
        compiler_params=_cparams(("parallel",)),
        name="rms_cast",
    )(x2, g.reshape(1, d))


def _proj_rope_kernel(h_ref, w_ref, cos_ref, sa_ref, sb_ref, z_ref):
    acc = jnp.dot(h_ref[...], w_ref[...].astype(BF16), preferred_element_type=F32)
    tn = acc.shape[1]
    rep = tn // HEAD_DIM
    c = jnp.concatenate([cos_ref[0]] * rep, axis=1)
    sa = jnp.concatenate([sa_ref[0]] * rep, axis=1)
    sb = jnp.concatenate([sb_ref[0]] * rep, axis=1)
    half = ROT_DIM // 2
    r = acc * c + pltpu.roll(acc, tn - half, 1) * sa + pltpu.roll(acc, half, 1) * sb
    z_ref[...] = r.astype(BF16)


def _proj_plain_kernel(h_ref, w_ref, z_ref):
    z_ref[...] = jnp.dot(h_ref[...], w_ref[...].astype(BF16), preferred_element_type=F32).astype(BF16)


def _proj_sigmoid_kernel(h_ref, w_ref, z_ref):
    acc = jnp.dot(h_ref[...], w_ref[...].astype(BF16), preferred_element_type=F32)
    z_ref[...] = jax.nn.sigmoid(acc).astype(BF16)


def _proj(h, w, col_start, width, *, tm, tn, name, rope=None, seq=None, q_width=None, sigmoid=False):
    n, d = h.shape
    assert n % tm == 0 and width % tn == 0 and col_start % tn == 0
    col_off = col_start // tn
    in_specs = [pl.BlockSpec((tm, d), lambda i, j: (i, 0)), pl.BlockSpec((d, tn), lambda i, j: (0, j + col_off))]
    args = [h, w]
    if rope is not None:
        assert seq % tm == 0 and q_width % tn == 0 and tn % HEAD_DIM == 0
        s_tiles, q_tiles = seq // tm, q_width // tn
        in_specs += [pl.BlockSpec((1, tm, HEAD_DIM), lambda i, j: (j // q_tiles, i % s_tiles, 0))] * 3
        args += list(rope)
        kern = _proj_rope_kernel
    else:
        kern = _proj_sigmoid_kernel if sigmoid else _proj_plain_kernel
    return pl.pallas_call(
        kern,
        out_shape=jax.ShapeDtypeStruct((n, width), BF16),
        grid=(n // tm, width // tn),
        in_specs=in_specs,
        out_specs=pl.BlockSpec((tm, tn), lambda i, j: (i, j)),
        compiler_params=_cparams(("parallel", "arbitrary")),
        name=name,
    )(*args)


_NT = (((1,), (1,)), ((), ()))


def _lane_fold(x, op, ways=4):
    n = x.shape[1] // V7X_LANES
    ways = min(ways, n)
    parts = [x[:, i * V7X_LANES:(i + 1) * V7X_LANES] for i in range(ways)]
    for i in range(ways, n):
        parts[i % ways] = op(parts[i % ways], x[:, i * V7X_LANES:(i + 1) * V7X_LANES])
    while len(parts) > 1:
        parts = [op(parts[2 * i], parts[2 * i + 1]) for i in range(len(parts) // 2)] + parts[len(parts) // 2 * 2:]
    return parts[0]


def _tile_rows(tile, tq):
    if isinstance(tile, int):
        return slice(tile * tq, (tile + 1) * tq)
    return pl.ds(pl.multiple_of(tile * tq, tq), tq)


def _attn_kernel(lamv_ref, g_ref, q_ref, k_ref, v_ref, o_ref, s_ref, p_ref, m_ref, mb_ref, l_ref, r_ref, il_ref,
                 *, tq, n_steps):
    seq = k_ref.shape[1]
    n_tiles = seq // tq
    tk = seq // n_steps
    tr = tq // n_steps
    lv = lamv_ref[...]
    lam = (jnp.exp(jnp.sum(lv[0:1] * lv[1:2], axis=-1, keepdims=True))
           - jnp.exp(jnp.sum(lv[2:3] * lv[3:4], axis=-1, keepdims=True)) + LAMBDA_INIT)
    gain = g_ref[...] * (1.0 - LAMBDA_INIT)

    def qk_piece(tile, slot, j):
        rows = _tile_rows(tile, tq)
        cols = slice(j * tk, (j + 1) * tk)
        for c in range(2):
            qc = q_ref[0, rows, c * HEAD_DIM:(c + 1) * HEAD_DIM]
            kc = k_ref[0, cols, c * HEAD_DIM:(c + 1) * HEAD_DIM]
            s = lax.dot_general(qc, kc, _NT, preferred_element_type=F32)
            s_ref[slot, c, :, cols] = s
            mx = _lane_fold(s, jnp.maximum)
            m_ref[slot, c] = mx if j == 0 else jnp.maximum(m_ref[slot, c], mx)

    def exp_prep(slot):
        for c in range(2):
            m = jnp.max(m_ref[slot, c], axis=-1, keepdims=True)
            mb_ref[c] = jnp.broadcast_to(m, (tq, V7X_LANES))

    def exp_piece(slot, i, pace):
        rs = slice(i * tr, (i + 1) * tr)
        for c in range(2):
            mrow = mb_ref[c, rs, :]
            if pace is not None:
                mrow = mrow + 0.0 * jnp.concatenate([pace] * (tr // V7X_SUBLANES), axis=0)
            m = jnp.concatenate([mrow] * (seq // V7X_LANES), axis=1)
            p = jnp.exp2(s_ref[slot, c, rs, :] - m)
            p_ref[slot, c, rs, :] = p.astype(BF16)
            l_ref[slot, c, rs, :] = _lane_fold(p, jnp.add)

    def pv_prep(slot):
        l1 = jnp.sum(l_ref[slot, 0], axis=-1, keepdims=True)
        l2 = jnp.sum(l_ref[slot, 1], axis=-1, keepdims=True)
        r_ref[...] = jnp.broadcast_to(lam * l1 / l2, (tq, V7X_LANES)).astype(BF16)
        il_ref[...] = jnp.broadcast_to(1.0 / l1, (tq, V7X_LANES))

    def pv_piece(slot, i, acc):
        cols = slice(i * tk, (i + 1) * tk)
        r = jnp.concatenate([r_ref[...]] * (tk // V7X_LANES), axis=1)
        a = p_ref[slot, 0, :, cols] - r * p_ref[slot, 1, :, cols]
        d = jnp.dot(a, v_ref[0, cols, :], preferred_element_type=F32)
        return d if acc is None else acc + d

    def pv_finish(tile, acc):
        o = acc * jnp.concatenate([il_ref[...]] * (acc.shape[1] // V7X_LANES), axis=1)
        ms = jnp.mean(o * o, axis=-1, keepdims=True)
        o_ref[0, _tile_rows(tile, tq), :] = (o * lax.rsqrt(ms + RMS_EPS) * gain).astype(BF16)

    def period(p, par, do_qk, do_exp, do_pv):
        if do_exp:
            exp_prep(1 - par)
        if do_pv:
            pv_prep(par)
        acc = None
        pace = None
        for i in range(n_steps):
            if do_qk:
                qk_piece(p, par, i)
            if do_exp:
                exp_piece(1 - par, i, pace)
            if do_pv:
                acc = pv_piece(par, i, acc)
                pace = acc[0:V7X_SUBLANES, 0:V7X_LANES]
        if do_pv:
            pv_finish(p - 2, acc)

    period(0, 0, True, False, False)
    period(1, 1, True, True, False)

    def body(g, carry):
        period(2 * g + 2, 0, True, True, True)
        period(2 * g + 3, 1, True, True, True)
        return carry

    lax.fori_loop(0, (n_tiles - 2) // 2, body, 0)
    period(n_tiles, 0, False, True, True)
    period(n_tiles + 1, 1, False, False, True)


def _attention(qk3, vf3, lamv, g_subln, *, n_heads, tq, n_steps):
    b, s, _ = qk3.shape
    vd = 2 * HEAD_DIM
    assert s % (2 * tq) == 0 and (s // n_steps) % V7X_LANES == 0 and (tq // n_steps) % 16 == 0
    return pl.pallas_call(
        functools.partial(_attn_kernel, tq=tq, n_steps=n_steps),
        out_shape=jax.ShapeDtypeStruct((b, s, n_heads * vd), BF16),
        grid=(b, n_heads),
        in_specs=[
            pl.BlockSpec((4, HEAD_DIM), lambda bi, h: (0, 0)),
            pl.BlockSpec((1, vd), lambda bi, h: (0, 0)),
            pl.BlockSpec((1, s, vd), lambda bi, h: (bi, 0, h)),
            pl.BlockSpec((1, s, vd), lambda bi, h: (bi, 0, n_heads + h)),
            pl.BlockSpec((1, s, vd), lambda bi, h: (bi, 0, h)),
        ],
        out_specs=pl.BlockSpec((1, s, vd), lambda bi, h: (bi, 0, h)),
        scratch_shapes=[
            pltpu.VMEM((2, 2, tq, s), F32),
            pltpu.VMEM((2, 2, tq, s), BF16),
            pltpu.VMEM((2, 2, tq, V7X_LANES), F32),
            pltpu.VMEM((2, tq, V7X_LANES), F32),
            pltpu.VMEM((2, 2, tq, V7X_LANES), F32),
            pltpu.VMEM((tq, V7X_LANES), BF16),
            pltpu.VMEM((tq, V7X_LANES), F32),
        ],
        compiler_params=_cparams(("parallel", "parallel")),
        name="diff_attn",
    )(lamv, g_subln.reshape(1, vd), qk3, qk3, vf3)


def _cmul_const(x, wr, wi):
    xr, xi = x
    wr = 0.0 if abs(wr) < 1e-12 else wr
    wi = 0.0 if abs(wi) < 1e-12 else wi
    if wi == 0.0:
        return x if wr == 1.0 else ((-xr, -xi) if wr == -1.0 else (xr * wr, xi * wr))
    if wr == 0.0:
        return (xi, -xr) if wi == -1.0 else ((-xi, xr) if wi == 1.0 else (-xi * wi, xr * wi))
    return (xr * wr - xi * wi, xr * wi + xi * wr)


def _fft_list(xs):
    n = len(xs)
    if n == 1:
        return xs
    ev = _fft_list(xs[0::2])
    od = _fft_list(xs[1::2])
    out = [None] * n
    for k in range(n // 2):
        ang = -2.0 * math.pi * k / n
        tr, ti = _cmul_const(od[k], math.cos(ang), math.sin(ang))
        out[k] = (ev[k][0] + tr, ev[k][1] + ti)
        out[k + n // 2] = (ev[k][0] - tr, ev[k][1] - ti)
    return out


def _fourier_kernel(f_ref, cd_ref, twc_ref, tws_ref, cs_ref, out_ref, g_ref):
    _, n1, n2, c = f_ref.shape
    n_groups = c // FOURIER_GROUP_DIM
    f2 = f_ref[0].reshape(n1 * n2, c)
    for grp in range(n_groups):
        lo = grp * FOURIER_GROUP_DIM
        gg = jnp.dot(f2[:, lo:lo + FOURIER_GROUP_DIM], cd_ref[...], preferred_element_type=F32)
        g_ref[:, :, lo:lo + FOURIER_GROUP_DIM] = gg[:, :FOURIER_GROUP_DIM].reshape(n1, n2, FOURIER_GROUP_DIM)
        g_ref[:, :, c + lo:c + lo + FOURIER_GROUP_DIM] = gg[:, FOURIER_GROUP_DIM:].reshape(n1, n2, FOURIER_GROUP_DIM)

    def body(t, carry):
        r = pl.multiple_of(t * V7X_SUBLANES, V7X_SUBLANES)
        rows = pl.ds(r, V7X_SUBLANES)
        for lc in range(c // V7X_LANES):
            re_l = slice(lc * V7X_LANES, (lc + 1) * V7X_LANES)
            im_l = slice(c + lc * V7X_LANES, c + (lc + 1) * V7X_LANES)
            ys = _fft_list([(g_ref[a, rows, re_l], g_ref[a, rows, im_l]) for a in range(n1)])
            for k1 in range(n1):
                yr, yi = ys[k1]
                tc = twc_ref[k1, rows, :]
                ts = tws_ref[k1, rows, :]
                g_ref[k1, rows, re_l] = yr * tc + yi * ts
                g_ref[k1, rows, im_l] = yi * tc - yr * ts
        return carry

    lax.fori_loop(0, n2 // V7X_SUBLANES, body, 0)

    cc = cs_ref[:, :n2]
    ss = cs_ref[:, n2:]
    for k1 in range(n1):
        zr = g_ref[k1, :, :c].astype(BF16)
        zi = g_ref[k1, :, c:].astype(BF16)
        xk = (jnp.dot(cc, zr, preferred_element_type=F32) + jnp.dot(ss, zi, preferred_element_type=F32))
        out_ref[0, :, k1 * c:(k1 + 1) * c] = xk.astype(BF16)


def _fourier(z4, cd, twc, tws, cs, *, f_col_block, c):
    b, n1, n2, _ = z4.shape
    return pl.pallas_call(
        _fourier_kernel,
        out_shape=jax.ShapeDtypeStruct((b, n2, n1 * c), BF16),
        grid=(b,),
        in_specs=[
            pl.BlockSpec((1, n1, n2, c), lambda bi: (bi, 0, 0, f_col_block)),
            pl.BlockSpec(cd.shape, lambda bi: (0, 0)),
            pl.BlockSpec(twc.shape, lambda bi: (0, 0, 0)),
            pl.BlockSpec(tws.shape, lambda bi: (0, 0, 0)),
            pl.BlockSpec(cs.shape, lambda bi: (0, 0)),
        ],
        out_specs=pl.BlockSpec((1, n2, n1 * c), lambda bi: (bi, 0, 0)),
        scratch_shapes=[pltpu.VMEM((n1, n2, 2 * c), F32)],
        compiler_params=_cparams(("parallel",)),
        name="fourier_mix",
    )(z4, cd, twc, tws, cs)


def _fourier_tables(seq):
    n1, n2 = FFT_RADIX, seq // FFT_RADIX
    gd = FOURIER_GROUP_DIM

    def cos_sin(n, scale):
        idx = jnp.arange(n, dtype=jnp.int32)
        ang = ((idx[:, None] * idx[None, :]) % n).astype(F32) * (2.0 * math.pi / n)
        return jnp.cos(ang) * scale, jnp.sin(ang) * scale

    cg, sg = cos_sin(gd, gd ** -0.5)
    cd = jnp.concatenate([cg, -sg], axis=1).astype(BF16)
    c2, s2 = cos_sin(n2, seq ** -0.5)
    cs = jnp.concatenate([c2, s2], axis=1).astype(BF16)
    k1 = jnp.arange(n1, dtype=jnp.int32)[:, None]
    r = jnp.arange(n2, dtype=jnp.int32)[None, :]
    ang = (k1 * r).astype(F32) * (2.0 * math.pi / seq)
    twc = jnp.broadcast_to(jnp.cos(ang)[:, :, None], (n1, n2, V7X_LANES))
    tws = jnp.broadcast_to(jnp.sin(ang)[:, :, None], (n1, n2, V7X_LANES))
    return cd, twc, tws, cs


def _merge_kernel(o_ref, fm_ref, ga0_ref, ga1_ref, gf0_ref, gf1_ref, x_ref, wa_ref, wf_ref, wo_ref,
                  g_ref, x1_ref, h2_ref):
    o = o_ref[...]
    fm = fm_ref[...]
    acc = x_ref[...]
    half = ga0_ref.shape[1]
    for ci, (ga_ref, gf_ref) in enumerate(((ga0_ref, gf0_ref), (ga1_ref, gf1_ref))):
        cols = slice(ci * half, (ci + 1) * half)
        br_a = jnp.dot(o, wa_ref[:, cols], preferred_element_type=F32)
        br_f = jnp.dot(fm, wf_ref[:, cols], preferred_element_type=F32)
        m = ga_ref[...].astype(F32) * br_a + gf_ref[...].astype(F32) * br_f
        acc = acc + jnp.dot(m.astype(BF16), wo_ref[cols, :], preferred_element_type=F32)
    x1_ref[...] = acc
    ms = jnp.mean(acc * acc, axis=-1, keepdims=True)
    h2_ref[...] = (acc * lax.rsqrt(ms + RMS_EPS) * g_ref[...]).astype(BF16)


def _merge(o2, fm2, gates, x2, wa, wf, wo, g_ffn, *, tm):
    n, d = x2.shape
    half = d // 2
    assert gates.shape[1] == 2 * d and n % tm == 0
    const = lambda i: (0, 0)
    return pl.pallas_call(
        _merge_kernel,
        out_shape=(jax.ShapeDtypeStruct((n, d), F32), jax.ShapeDtypeStruct((n, d), BF16)),
        grid=(n // tm,),
        in_specs=[
            pl.BlockSpec((tm, o2.shape[1]), lambda i: (i, 0)),
            pl.BlockSpec((tm, fm2.shape[1]), lambda i: (i, 0)),
            pl.BlockSpec((tm, half), lambda i: (i, 0)),
            pl.BlockSpec((tm, half), lambda i: (i, 1)),
            pl.BlockSpec((tm, half), lambda i: (i, 2)),
            pl.BlockSpec((tm, half), lambda i: (i, 3)),
            pl.BlockSpec((tm, d), lambda i: (i, 0)),
            pl.BlockSpec(wa.shape, const, pipeline_mode=pl.Buffered(1)),
            pl.BlockSpec(wf.shape, const, pipeline_mode=pl.Buffered(1)),
            pl.BlockSpec(wo.shape, const, pipeline_mode=pl.Buffered(1)),
            pl.BlockSpec((1, d), const),
        ],
        out_specs=(pl.BlockSpec((tm, d), lambda i: (i, 0)), pl.BlockSpec((tm, d), lambda i: (i, 0))),
        compiler_params=_cparams(("parallel",)),
        name="merge_out_proj",
    )(o2, fm2, gates, gates, gates, gates, x2, wa, wf, wo, g_ffn.reshape(1, d))


def _ffn_kernel(h_ref, x1_ref, wg_ref, wu_ref, wd_ref, g_ref, out_ref):
    j = pl.program_id(1)

    @pl.when(j == 0)
    def _():
        out_ref[...] = x1_ref[...]

    h = h_ref[...]
    gate = jnp.dot(h, wg_ref[...], preferred_element_type=F32)
    up = jnp.dot(h, wu_ref[...], preferred_element_type=F32)
    act = (gate * jax.nn.sigmoid(gate) * up).astype(BF16)
    out_ref[...] += jnp.dot(act, wd_ref[...], preferred_element_type=F32)

    @pl.when(j == pl.num_programs(1) - 1)
    def _():
        y = out_ref[...]
        ms = jnp.mean(y * y, axis=-1, keepdims=True)
        out_ref[...] = y * lax.rsqrt(ms + RMS_EPS) * g_ref[...]


def _ffn(h2, x1, wg, wu, wd, g_final, *, tm, tf):
    n, d = x1.shape
    dff = wg.shape[1]
    assert n % tm == 0 and dff % tf == 0
    return pl.pallas_call(
        _ffn_kernel,
        out_shape=jax.ShapeDtypeStruct((n, d), F32),
        grid=(n // tm, dff // tf),
        in_specs=[
            pl.BlockSpec((tm, d), lambda i, j: (i, 0)),
            pl.BlockSpec((tm, d), lambda i, j: (i, 0)),
            pl.BlockSpec((d, tf), lambda i, j: (0, j)),
            pl.BlockSpec((d, tf), lambda i, j: (0, j)),
            pl.BlockSpec((tf, d), lambda i, j: (j, 0)),
            pl.BlockSpec((1, d), lambda i, j: (0, 0)),
        ],
        out_specs=pl.BlockSpec((tm, d), lambda i, j: (i, 0)),
        compiler_params=_cparams(("parallel", "arbitrary")),
        name="swiglu_ffn",
    )(h2, x1, wg, wu, wd, g_final.reshape(1, d))


def _rope_tables(seq):
    half = ROT_DIM // 2
    pos = jnp.arange(seq, dtype=F32)
    inv_freq = ROPE_THETA ** (-jnp.arange(0, ROT_DIM, 2, dtype=F32) / ROT_DIM)
    ang = pos[:, None] * inv_freq[None, :]
    cos, sin = jnp.cos(ang), jnp.sin(ang)
    ones = jnp.ones((seq, HEAD_DIM - ROT_DIM), F32)
    zeros = lambda w: jnp.zeros((seq, w), F32)
    rope_c = jnp.concatenate([cos, cos, ones], axis=1)
    rope_sa = jnp.concatenate([-sin, zeros(HEAD_DIM - half)], axis=1)
    rope_sb = jnp.concatenate([zeros(half), sin, zeros(HEAD_DIM - ROT_DIM)], axis=1)
    q_scale = math.log2(math.e) / math.sqrt(HEAD_DIM)
    return tuple(jnp.stack([t * q_scale, t]) for t in (rope_c, rope_sa, rope_sb))


def _tile(n, pref):
    t = min(n, pref)
    while n % t:
        t //= 2
    return t


def kernel(x, g_mix, w_in, lambda_q1, lambda_k1, lambda_q2, lambda_k2, g_subln, w_attn_branch,
           w_four_branch, w_out, g_ffn, w_gate, w_up, w_down, g_final):
    b, s, d = x.shape
    n = b * s
    v_width = w_attn_branch.shape[0]
    f_width = w_four_branch.shape[0]
    vd = g_subln.shape[0]
    n_heads = v_width // vd
    qk_width = n_heads * 2 * HEAD_DIM
    f_start = 2 * qk_width + v_width
    gate_start = f_start + f_width
    assert vd == 2 * HEAD_DIM and w_in.shape[1] == gate_start + 2 * d
    assert v_width % f_width == 0 and s % (FFT_RADIX * 2 * V7X_SUBLANES) == 0

    x2 = x.reshape(n, d)
    tm = _tile(s, 1024)
    h = _rms_cast(x2, g_mix, tm=tm)
    qk = _proj(h, w_in, 0, 2 * qk_width, tm=tm, tn=qk_width, name="proj_qk_rope",
               rope=_rope_tables(s), seq=s, q_width=qk_width)
    vf = _proj(h, w_in, 2 * qk_width, v_width + f_width, tm=tm, tn=1024, name="proj_vf")
    gates = _proj(h, w_in, gate_start, 2 * d, tm=tm, tn=1024, name="proj_gates", sigmoid=True)

    lamv = jnp.stack([lambda_q1, lambda_k1, lambda_q2, lambda_k2]).astype(F32)
    o = _attention(qk.reshape(b, s, -1), vf.reshape(b, s, -1), lamv, g_subln, n_heads=n_heads, tq=256,
                   n_steps=min(16, s // 256))

    cd, twc, tws, cs = _fourier_tables(s)
    fm = _fourier(vf.reshape(b, FFT_RADIX, s // FFT_RADIX, -1), cd, twc, tws, cs,
                  f_col_block=v_width // f_width, c=f_width)

    x1, h2 = _merge(o.reshape(n, v_width), fm.reshape(n, f_width), gates, x2, w_attn_branch.astype(BF16),
                    w_four_branch.astype(BF16), w_out.astype(BF16), g_ffn, tm=_tile(n, 512))

    out = _ffn(h2, x1, w_gate.astype(BF16), w_up.astype(BF16), w_down.astype(BF16), g_final,
               tm=_tile(n, 1024), tf=512)
    return out.reshape(b, s, d)
```

```python
import functools
import math

import jax
import jax.numpy as jnp
from jax import lax
from jax.experimental import pallas as pl
from jax.experimental.pallas import tpu as pltpu

F32 = jnp.float32
BF16 = jnp.bfloat16

HEAD_DIM = 128
ROT_DIM = HEAD_DIM // 4
ROPE_THETA = 500000.0
LAMBDA_INIT = 0.8 - 0.6 * math.exp(-0.3 * 0)
FOURIER_GROUP_DIM = 128
RMS_EPS = 1e-5
FFT_RADIX = 16

V7X_LANES = 128
V7X_SUBLANES = 8
V7X_VMEM_LIMIT_BYTES = 58 * 1024 * 1024


def _cparams(sem):
    return pltpu.CompilerParams(dimension_semantics=sem, vmem_limit_bytes=V7X_VMEM_LIMIT_BYTES)


def _proj_rope_kernel(h_ref, w_ref, cos_ref, sa_ref, sb_ref, z_ref):
    acc = jnp.dot(h_ref[...], w_ref[...].astype(BF16), preferred_element_type=F32)
    tn = acc.shape[1]
    rep = tn // HEAD_DIM
    c = jnp.concatenate([cos_ref[0]] * rep, axis=1)
    sa = jnp.concatenate([sa_ref[0]] * rep, axis=1)
    sb = jnp.concatenate([sb_ref[0]] * rep, axis=1)
    half = ROT_DIM // 2
    r = acc * c + pltpu.roll(acc, tn - half, 1) * sa + pltpu.roll(acc, half, 1) * sb
    z_ref[...] = r.astype(BF16)


def _proj_norm_kernel(x_ref, g_ref, w_ref, z_ref, h_ref):
    @pl.when(pl.program_id(1) == 0)
    def _():
        x = x_ref[...]
        ms = jnp.mean(x * x, axis=-1, keepdims=True)
        h_ref[...] = (x * lax.rsqrt(ms + RMS_EPS) * g_ref[...]).astype(BF16)

    z_ref[...] = jnp.dot(h_ref[...], w_ref[...].astype(BF16), preferred_element_type=F32).astype(BF16)


def _proj_norm(x2, g, w, col_start, width, *, tm, tn):
    n, d = x2.shape
    assert n % tm == 0 and width % tn == 0 and col_start % tn == 0
    col_off = col_start // tn
    return pl.pallas_call(
        _proj_norm_kernel,
        out_shape=(jax.ShapeDtypeStruct((n, width), BF16), jax.ShapeDtypeStruct((n, d), BF16)),
        grid=(n // tm, width // tn),
        in_specs=[pl.BlockSpec((tm, d), lambda i, j: (i, 0)), pl.BlockSpec((1, d), lambda i, j: (0, 0)),
                  pl.BlockSpec((d, tn), lambda i, j: (0, j + col_off))],
        out_specs=(pl.BlockSpec((tm, tn), lambda i, j: (i, j)), pl.BlockSpec((tm, d), lambda i, j: (i, 0))),
        compiler_params=_cparams(("parallel", "arbitrary")),
        name="proj_vf_norm",
    )(x2, g.reshape(1, d), w)


def _proj_sigmoid_kernel(h_ref, w_ref, wg_ref, wu_ref, wd_ref, z_ref, wg_out, wu_out, wd_out):
    wg_out[...] = wg_ref[...].astype(BF16)
    wu_out[...] = wu_ref[...].astype(BF16)

    @pl.when(pl.program_id(1) == 0)
    def _():
        wd_out[...] = wd_ref[...].astype(BF16)

    acc = jnp.dot(h_ref[...], w_ref[...].astype(BF16), preferred_element_type=F32)
    z_ref[...] = jax.nn.sigmoid(acc).astype(BF16)


def _proj_gates(h, w, col_start, width, w_gate, w_up, w_down, *, tm, tn):
    n, d = h.shape
    dff = w_gate.shape[1]
    ni, nj = n // tm, width // tn
    assert n % tm == 0 and width % tn == 0 and col_start % tn == 0
    rg, rd = d // (ni * nj), dff // ni
    assert rg * ni * nj == d and rd * ni == dff and rg % 16 == 0 and rd % 16 == 0
    col_off = col_start // tn
    step = lambda i, j: (i * nj + j, 0)
    return pl.pallas_call(
        _proj_sigmoid_kernel,
        out_shape=(jax.ShapeDtypeStruct((n, width), BF16), jax.ShapeDtypeStruct(w_gate.shape, BF16),
                   jax.ShapeDtypeStruct(w_up.shape, BF16), jax.ShapeDtypeStruct(w_down.shape, BF16)),
        grid=(ni, nj),
        in_specs=[pl.BlockSpec((tm, d), lambda i, j: (i, 0)), pl.BlockSpec((d, tn), lambda i, j: (0, j + col_off)),
                  pl.BlockSpec((rg, dff), step), pl.BlockSpec((rg, dff), step),
                  pl.BlockSpec((rd, d), lambda i, j: (i, 0))],
        out_specs=(pl.BlockSpec((tm, tn), lambda i, j: (i, j)), pl.BlockSpec((rg, dff), step),
                   pl.BlockSpec((rg, dff), step), pl.BlockSpec((rd, d), lambda i, j: (i, 0))),
        compiler_params=_cparams(("arbitrary", "arbitrary")),
        name="proj_gates",
    )(h, w, w_gate, w_up, w_down)


def _proj_rope(h, w, col_start, width, rope, *, seq, q_width, tm, tn):
    n, d = h.shape
    assert n % tm == 0 and width % tn == 0 and col_start % tn == 0
    assert seq % tm == 0 and q_width % tn == 0 and tn % HEAD_DIM == 0
    col_off = col_start // tn
    s_tiles, q_tiles = seq // tm, q_width // tn
    table = pl.BlockSpec((1, tm, HEAD_DIM), lambda i, j: (j // q_tiles, i % s_tiles, 0))
    return pl.pallas_call(
        _proj_rope_kernel,
        out_shape=jax.ShapeDtypeStruct((n, width), BF16),
        grid=(n // tm, width // tn),
        in_specs=[pl.BlockSpec((tm, d), lambda i, j: (i, 0)), pl.BlockSpec((d, tn), lambda i, j: (0, j + col_off)),
                  table, table, table],
        out_specs=pl.BlockSpec((tm, tn), lambda i, j: (i, j)),
        compiler_params=_cparams(("parallel", "arbitrary")),
        name="proj_qk_rope",
    )(h, w, *rope)


_NT = (((1,), (1,)), ((), ()))


def _lane_fold(x, op, ways=4):
    n = x.shape[1] // V7X_LANES
    ways = min(ways, n)
    parts = [x[:, i * V7X_LANES:(i + 1) * V7X_LANES] for i in range(ways)]
    for i in range(ways, n):
        parts[i % ways] = op(parts[i % ways], x[:, i * V7X_LANES:(i + 1) * V7X_LANES])
    while len(parts) > 1:
        parts = [op(parts[2 * i], parts[2 * i + 1]) for i in range(len(parts) // 2)] + parts[len(parts) // 2 * 2:]
    return parts[0]


def _tile_rows(tile, tq):
    if isinstance(tile, int):
        return slice(tile * tq, (tile + 1) * tq)
    return pl.ds(pl.multiple_of(tile * tq, tq), tq)


def _attn_kernel(lamv_ref, g_ref, q_ref, k_ref, v_ref, o_ref, s_ref, p_ref, m_ref, mb_ref, l_ref, r_ref, il_ref,
                 *, tq, n_steps):
    seq = k_ref.shape[1]
    n_tiles = seq // tq
    tk = seq // n_steps
    tr = tq // n_steps
    lv = lamv_ref[...]
    lam = (jnp.exp(jnp.sum(lv[0:1] * lv[1:2], axis=-1, keepdims=True))
           - jnp.exp(jnp.sum(lv[2:3] * lv[3:4], axis=-1, keepdims=True)) + LAMBDA_INIT)
    gain = g_ref[...] * (1.0 - LAMBDA_INIT)

    def qk_piece(tile, slot, j):
        rows = _tile_rows(tile, tq)
        cols = slice(j * tk, (j + 1) * tk)
        for c in range(2):
            qc = q_ref[0, rows, c * HEAD_DIM:(c + 1) * HEAD_DIM]
            kc = k_ref[0, cols, c * HEAD_DIM:(c + 1) * HEAD_DIM]
            s = lax.dot_general(qc, kc, _NT, preferred_element_type=F32)
            s_ref[slot, c, :, cols] = s
            mx = _lane_fold(s, jnp.maximum)
            m_ref[slot, c] = mx if j == 0 else jnp.maximum(m_ref[slot, c], mx)

    def exp_prep(slot):
        for c in range(2):
            m = jnp.max(m_ref[slot, c], axis=-1, keepdims=True)
            mb_ref[c] = jnp.broadcast_to(m, (tq, V7X_LANES))

    def exp_piece(slot, i, pace):
        rs = slice(i * tr, (i + 1) * tr)
        for c in range(2):
            mrow = mb_ref[c, rs, :]
            if pace is not None:
                mrow = mrow + 0.0 * jnp.concatenate([pace] * (tr // V7X_SUBLANES), axis=0)
            m = jnp.concatenate([mrow] * (seq // V7X_LANES), axis=1)
            p = jnp.exp2(s_ref[slot, c, rs, :] - m)
            p_ref[slot, c, rs, :] = p.astype(BF16)
            l_ref[slot, c, rs, :] = _lane_fold(p, jnp.add)

    def pv_prep(slot):
        l1 = jnp.sum(l_ref[slot, 0], axis=-1, keepdims=True)
        l2 = jnp.sum(l_ref[slot, 1], axis=-1, keepdims=True)
        r_ref[...] = jnp.broadcast_to(lam * l1 / l2, (tq, V7X_LANES)).astype(BF16)
        il_ref[...] = jnp.broadcast_to(1.0 / l1, (tq, V7X_LANES))

    def pv_piece(slot, i, acc):
        cols = slice(i * tk, (i + 1) * tk)
        r = jnp.concatenate([r_ref[...]] * (tk // V7X_LANES), axis=1)
        a = p_ref[slot, 0, :, cols] - r * p_ref[slot, 1, :, cols]
        d = jnp.dot(a, v_ref[0, cols, :], preferred_element_type=F32)
        return d if acc is None else acc + d

    def pv_finish(tile, acc):
        o = acc * jnp.concatenate([il_ref[...]] * (acc.shape[1] // V7X_LANES), axis=1)
        ms = jnp.mean(o * o, axis=-1, keepdims=True)
        o_ref[0, _tile_rows(tile, tq), :] = (o * lax.rsqrt(ms + RMS_EPS) * gain).astype(BF16)

    def period(p, par, do_qk, do_exp, do_pv):
        if do_exp:
            exp_prep(1 - par)
        if do_pv:
            pv_prep(par)
        acc = None
        pace = None
        for i in range(n_steps):
            if do_qk:
                qk_piece(p, par, i)
            if do_exp:
                exp_piece(1 - par, i, pace)
            if do_pv:
                acc = pv_piece(par, i, acc)
                pace = acc[0:V7X_SUBLANES, 0:V7X_LANES]
        if do_pv:
            pv_finish(p - 2, acc)

    period(0, 0, True, False, False)
    period(1, 1, True, True, False)

    def body(g, carry):
        period(2 * g + 2, 0, True, True, True)
        period(2 * g + 3, 1, True, True, True)
        return carry

    lax.fori_loop(0, (n_tiles - 2) // 2, body, 0)
    period(n_tiles, 0, False, True, True)
    period(n_tiles + 1, 1, False, False, True)


def _attention(qk3, vf3, lamv, g_subln, *, n_heads, tq, n_steps):
    b, s, _ = qk3.shape
    vd = 2 * HEAD_DIM
    assert s % (2 * tq) == 0 and (s // n_steps) % V7X_LANES == 0 and (tq // n_steps) % 16 == 0
    return pl.pallas_call(
        functools.partial(_attn_kernel, tq=tq, n_steps=n_steps),
        out_shape=jax.ShapeDtypeStruct((b, s, n_heads * vd), BF16),
        grid=(b, n_heads),
        in_specs=[
            pl.BlockSpec((4, HEAD_DIM), lambda bi, h: (0, 0)),
            pl.BlockSpec((1, vd), lambda bi, h: (0, 0)),
            pl.BlockSpec((1, s, vd), lambda bi, h: (bi, 0, h)),
            pl.BlockSpec((1, s, vd), lambda bi, h: (bi, 0, n_heads + h)),
            pl.BlockSpec((1, s, vd), lambda bi, h: (bi, 0, h)),
        ],
        out_specs=pl.BlockSpec((1, s, vd), lambda bi, h: (bi, 0, h)),
        scratch_shapes=[
            pltpu.VMEM((2, 2, tq, s), F32),
            pltpu.VMEM((2, 2, tq, s), BF16),
            pltpu.VMEM((2, 2, tq, V7X_LANES), F32),
            pltpu.VMEM((2, tq, V7X_LANES), F32),
            pltpu.VMEM((2, 2, tq, V7X_LANES), F32),
            pltpu.VMEM((tq, V7X_LANES), BF16),
            pltpu.VMEM((tq, V7X_LANES), F32),
        ],
        compiler_params=_cparams(("parallel", "parallel")),
        name="diff_attn",
    )(lamv, g_subln.reshape(1, vd), qk3, qk3, vf3)


def _cmul_const(x, wr, wi):
    xr, xi = x
    wr = 0.0 if abs(wr) < 1e-12 else wr
    wi = 0.0 if abs(wi) < 1e-12 else wi
    if wi == 0.0:
        return x if wr == 1.0 else ((-xr, -xi) if wr == -1.0 else (xr * wr, xi * wr))
    if wr == 0.0:
        return (xi, -xr) if wi == -1.0 else ((-xi, xr) if wi == 1.0 else (-xi * wi, xr * wi))
    return (xr * wr - xi * wi, xr * wi + xi * wr)


def _fft_list(xs):
    n = len(xs)
    if n == 1:
        return xs
    ev = _fft_list(xs[0::2])
    od = _fft_list(xs[1::2])
    out = [None] * n
    for k in range(n // 2):
        ang = -2.0 * math.pi * k / n
        tr, ti = _cmul_const(od[k], math.cos(ang), math.sin(ang))
        out[k] = (ev[k][0] + tr, ev[k][1] + ti)
        out[k + n // 2] = (ev[k][0] - tr, ev[k][1] - ti)
    return out


def _fourier_kernel(f_ref, cd_ref, twc_ref, tws_ref, cs_ref, out_ref, g_ref):
    _, n1, n2, c = f_ref.shape
    n_groups = c // FOURIER_GROUP_DIM
    f2 = f_ref[0].reshape(n1 * n2, c)
    for grp in range(n_groups):
        lo = grp * FOURIER_GROUP_DIM
        gg = jnp.dot(f2[:, lo:lo + FOURIER_GROUP_DIM], cd_ref[...], preferred_element_type=F32)
        g_ref[:, :, lo:lo + FOURIER_GROUP_DIM] = gg[:, :FOURIER_GROUP_DIM].reshape(n1, n2, FOURIER_GROUP_DIM)
        g_ref[:, :, c + lo:c + lo + FOURIER_GROUP_DIM] = gg[:, FOURIER_GROUP_DIM:].reshape(n1, n2, FOURIER_GROUP_DIM)

    def body(t, carry):
        r = pl.multiple_of(t * V7X_SUBLANES, V7X_SUBLANES)
        rows = pl.ds(r, V7X_SUBLANES)
        for lc in range(c // V7X_LANES):
            re_l = slice(lc * V7X_LANES, (lc + 1) * V7X_LANES)
            im_l = slice(c + lc * V7X_LANES, c + (lc + 1) * V7X_LANES)
            ys = _fft_list([(g_ref[a, rows, re_l], g_ref[a, rows, im_l]) for a in range(n1)])
            for k1 in range(n1):
                yr, yi = ys[k1]
                tc = twc_ref[k1, rows, :]
                ts = tws_ref[k1, rows, :]
                g_ref[k1, rows, re_l] = yr * tc + yi * ts
                g_ref[k1, rows, im_l] = yi * tc - yr * ts
        return carry

    lax.fori_loop(0, n2 // V7X_SUBLANES, body, 0)

    cc = cs_ref[:, :n2]
    ss = cs_ref[:, n2:]
    for k1 in range(n1):
        zr = g_ref[k1, :, :c].astype(BF16)
        zi = g_ref[k1, :, c:].astype(BF16)
        xk = (jnp.dot(cc, zr, preferred_element_type=F32) + jnp.dot(ss, zi, preferred_element_type=F32))
        out_ref[0, :, k1 * c:(k1 + 1) * c] = xk.astype(BF16)


def _fourier(z4, cd, twc, tws, cs, *, f_col_block, c):
    b, n1, n2, _ = z4.shape
    return pl.pallas_call(
        _fourier_kernel,
        out_shape=jax.ShapeDtypeStruct((b, n2, n1 * c), BF16),
        grid=(b,),
        in_specs=[
            pl.BlockSpec((1, n1, n2, c), lambda bi: (bi, 0, 0, f_col_block)),
            pl.BlockSpec(cd.shape, lambda bi: (0, 0)),
            pl.BlockSpec(twc.shape, lambda bi: (0, 0, 0)),
            pl.BlockSpec(tws.shape, lambda bi: (0, 0, 0)),
            pl.BlockSpec(cs.shape, lambda bi: (0, 0)),
        ],
        out_specs=pl.BlockSpec((1, n2, n1 * c), lambda bi: (bi, 0, 0)),
        scratch_shapes=[pltpu.VMEM((n1, n2, 2 * c), F32)],
        compiler_params=_cparams(("parallel",)),
        name="fourier_mix",
    )(z4, cd, twc, tws, cs)


def _fourier_tables(seq):
    n1, n2 = FFT_RADIX, seq // FFT_RADIX
    gd = FOURIER_GROUP_DIM

    def cos_sin(n, scale):
        idx = jnp.arange(n, dtype=jnp.int32)
        ang = ((idx[:, None] * idx[None, :]) % n).astype(F32) * (2.0 * math.pi / n)
        return jnp.cos(ang) * scale, jnp.sin(ang) * scale

    cg, sg = cos_sin(gd, gd ** -0.5)
    cd = jnp.concatenate([cg, -sg], axis=1).astype(BF16)
    c2, s2 = cos_sin(n2, seq ** -0.5)
    cs = jnp.concatenate([c2, s2], axis=1).astype(BF16)
    k1 = jnp.arange(n1, dtype=jnp.int32)[:, None]
    r = jnp.arange(n2, dtype=jnp.int32)[None, :]
    ang = (k1 * r).astype(F32) * (2.0 * math.pi / seq)
    twc = jnp.broadcast_to(jnp.cos(ang)[:, :, None], (n1, n2, V7X_LANES))
    tws = jnp.broadcast_to(jnp.sin(ang)[:, :, None], (n1, n2, V7X_LANES))
    return cd, twc, tws, cs


def _merge_kernel(o_ref, fm_ref, ga0_ref, ga1_ref, gf0_ref, gf1_ref, x_ref, wa_ref, wf_ref, wo_ref,
                  g_ref, x1_ref, h2_ref):
    o = o_ref[...]
    fm = fm_ref[...]
    acc = x_ref[...]
    half = ga0_ref.shape[1]
    for ci, (ga_ref, gf_ref) in enumerate(((ga0_ref, gf0_ref), (ga1_ref, gf1_ref))):
        cols = slice(ci * half, (ci + 1) * half)
        br_a = jnp.dot(o, wa_ref[:, cols], preferred_element_type=F32)
        br_f = jnp.dot(fm, wf_ref[:, cols], preferred_element_type=F32)
        m = ga_ref[...].astype(F32) * br_a + gf_ref[...].astype(F32) * br_f
        acc = acc + jnp.dot(m.astype(BF16), wo_ref[cols, :], preferred_element_type=F32)
    x1_ref[...] = acc
    ms = jnp.mean(acc * acc, axis=-1, keepdims=True)
    h2_ref[...] = (acc * lax.rsqrt(ms + RMS_EPS) * g_ref[...]).astype(BF16)


def _merge(o2, fm2, gates, x2, wa, wf, wo, g_ffn, *, tm):
    n, d = x2.shape
    half = d // 2
    assert gates.shape[1] == 2 * d and n % tm == 0
    const = lambda i: (0, 0)
    return pl.pallas_call(
        _merge_kernel,
        out_shape=(jax.ShapeDtypeStruct((n, d), F32), jax.ShapeDtypeStruct((n, d), BF16)),
        grid=(n // tm,),
        in_specs=[
            pl.BlockSpec((tm, o2.shape[1]), lambda i: (i, 0)),
            pl.BlockSpec((tm, fm2.shape[1]), lambda i: (i, 0)),
            pl.BlockSpec((tm, half), lambda i: (i, 0)),
            pl.BlockSpec((tm, half), lambda i: (i, 1)),
            pl.BlockSpec((tm, half), lambda i: (i, 2)),
            pl.BlockSpec((tm, half), lambda i: (i, 3)),
            pl.BlockSpec((tm, d), lambda i: (i, 0)),
            pl.BlockSpec(wa.shape, const, pipeline_mode=pl.Buffered(1)),
            pl.BlockSpec(wf.shape, const, pipeline_mode=pl.Buffered(1)),
            pl.BlockSpec(wo.shape, const, pipeline_mode=pl.Buffered(1)),
            pl.BlockSpec((1, d), const),
        ],
        out_specs=(pl.BlockSpec((tm, d), lambda i: (i, 0)), pl.BlockSpec((tm, d), lambda i: (i, 0))),
        compiler_params=_cparams(("parallel",)),
        name="merge_out_proj",
    )(o2, fm2, gates, gates, gates, gates, x2, wa, wf, wo, g_ffn.reshape(1, d))


def _ffn_kernel(h_ref, x1_ref, wg_ref, wu_ref, wd_ref, g_ref, out_ref):
    j = pl.program_id(1)

    @pl.when(j == 0)
    def _():
        out_ref[...] = x1_ref[...]

    h = h_ref[...]
    gate = jnp.dot(h, wg_ref[...], preferred_element_type=F32)
    up = jnp.dot(h, wu_ref[...], preferred_element_type=F32)
    act = (gate * jax.nn.sigmoid(gate) * up).astype(BF16)
    out_ref[...] += jnp.dot(act, wd_ref[...], preferred_element_type=F32)

    @pl.when(j == pl.num_programs(1) - 1)
    def _():
        y = out_ref[...]
        ms = jnp.mean(y * y, axis=-1, keepdims=True)
        out_ref[...] = y * lax.rsqrt(ms + RMS_EPS) * g_ref[...]


def _ffn(h2, x1, wg, wu, wd, g_final, *, tm, tf):
    n, d = x1.shape
    dff = wg.shape[1]
    assert n % tm == 0 and dff % tf == 0
    return pl.pallas_call(
        _ffn_kernel,
        out_shape=jax.ShapeDtypeStruct((n, d), F32),
        grid=(n // tm, dff // tf),
        in_specs=[
            pl.BlockSpec((tm, d), lambda i, j: (i, 0)),
            pl.BlockSpec((tm, d), lambda i, j: (i, 0)),
            pl.BlockSpec((d, tf), lambda i, j: (0, j)),
            pl.BlockSpec((d, tf), lambda i, j: (0, j)),
            pl.BlockSpec((tf, d), lambda i, j: (j, 0)),
            pl.BlockSpec((1, d), lambda i, j: (0, 0)),
        ],
        out_specs=pl.BlockSpec((tm, d), lambda i, j: (i, 0)),
        compiler_params=_cparams(("parallel", "arbitrary")),
        name="swiglu_ffn",
    )(h2, x1, wg, wu, wd, g_final.reshape(1, d))


def _rope_tables(seq):
    half = ROT_DIM // 2
    pos = jnp.arange(seq, dtype=F32)
    inv_freq = ROPE_THETA ** (-jnp.arange(0, ROT_DIM, 2, dtype=F32) / ROT_DIM)
    ang = pos[:, None] * inv_freq[None, :]
    cos, sin = jnp.cos(ang), jnp.sin(ang)
    ones = jnp.ones((seq, HEAD_DIM - ROT_DIM), F32)
    zeros = lambda w: jnp.zeros((seq, w), F32)
    rope_c = jnp.concatenate([cos, cos, ones], axis=1)
    rope_sa = jnp.concatenate([-sin, zeros(HEAD_DIM - half)], axis=1)
    rope_sb = jnp.concatenate([zeros(half), sin, zeros(HEAD_DIM - ROT_DIM)], axis=1)
    q_scale = math.log2(math.e) / math.sqrt(HEAD_DIM)
    return tuple(jnp.stack([t * q_scale, t]) for t in (rope_c, rope_sa, rope_sb))


def _tile(n, pref):
    t = min(n, pref)
    while n % t:
        t //= 2
    return t


def kernel(x, g_mix, w_in, lambda_q1, lambda_k1, lambda_q2, lambda_k2, g_subln, w_attn_branch,
           w_four_branch, w_out, g_ffn, w_gate, w_up, w_down, g_final):
    b, s, d = x.shape
    n = b * s
    v_width = w_attn_branch.shape[0]
    f_width = w_four_branch.shape[0]
    vd = g_subln.shape[0]
    n_heads = v_width // vd
    qk_width = n_heads * 2 * HEAD_DIM
    f_start = 2 * qk_width + v_width
    gate_start = f_start + f_width
    assert vd == 2 * HEAD_DIM and w_in.shape[1] == gate_start + 2 * d
    assert v_width % f_width == 0 and s % (FFT_RADIX * 2 * V7X_SUBLANES) == 0

    x2 = x.reshape(n, d)
    tm = _tile(s, 1024)
    vf, h = _proj_norm(x2, g_mix, w_in, 2 * qk_width, v_width + f_width, tm=tm, tn=512)
    qk = _proj_rope(h, w_in, 0, 2 * qk_width, _rope_tables(s), seq=s, q_width=qk_width, tm=tm, tn=qk_width)
    gates, wg_bf, wu_bf, wd_bf = _proj_gates(h, w_in, gate_start, 2 * d, w_gate, w_up, w_down, tm=tm, tn=1024)

    lamv = jnp.stack([lambda_q1, lambda_k1, lambda_q2, lambda_k2]).astype(F32)
    o = _attention(qk.reshape(b, s, -1), vf.reshape(b, s, -1), lamv, g_subln, n_heads=n_heads, tq=256,
                   n_steps=min(16, s // 256))

    cd, twc, tws, cs = _fourier_tables(s)
    fm = _fourier(vf.reshape(b, FFT_RADIX, s // FFT_RADIX, -1), cd, twc, tws, cs,
                  f_col_block=v_width // f_width, c=f_width)

    x1, h2 = _merge(o.reshape(n, v_width), fm.reshape(n, f_width), gates, x2, w_attn_branch.astype(BF16),
                    w_four_branch.astype(BF16), w_out.astype(BF16), g_ffn, tm=_tile(n, 512))

    out = _ffn(h2, x1, wg_bf, wu_bf, wd_bf, g_final, tm=_tile(n, 1024), tf=512)
    return out.reshape(b, s, d)
```

```python
import functools
import math

import jax
import jax.numpy as jnp
from jax import lax
from jax.experimental import pallas as pl
from jax.experimental.pallas import tpu as pltpu

F32 = jnp.float32
BF16 = jnp.bfloat16

HEAD_DIM = 128
ROT_DIM = HEAD_DIM // 4
ROPE_THETA = 500000.0
LAMBDA_INIT = 0.8 - 0.6 * math.exp(-0.3 * 0)
FOURIER_GROUP_DIM = 128
RMS_EPS = 1e-5
FFT_RADIX = 16

V7X_LANES = 128
V7X_SUBLANES = 8
V7X_VMEM_LIMIT_BYTES = 58 * 1024 * 1024


def _cparams(sem):
    return pltpu.CompilerParams(dimension_semantics=sem, vmem_limit_bytes=V7X_VMEM_LIMIT_BYTES)


def _proj_rope_kernel(h_ref, w_ref, cos_ref, sa_ref, sb_ref, z_ref):
    acc = jnp.dot(h_ref[...], w_ref[...].astype(BF16), preferred_element_type=F32)
    tn = acc.shape[1]
    rep = tn // HEAD_DIM
    c = jnp.concatenate([cos_ref[0]] * rep, axis=1)
    sa = jnp.concatenate([sa_ref[0]] * rep, axis=1)
    sb = jnp.concatenate([sb_ref[0]] * rep, axis=1)
    half = ROT_DIM // 2
    r = acc * c + pltpu.roll(acc, tn - half, 1) * sa + pltpu.roll(acc, half, 1) * sb
    z_ref[...] = r.astype(BF16)


def _proj_norm_kernel(x_ref, g_ref, *refs):
    w_refs, z_ref, h_ref = refs[:-2], refs[-2], refs[-1]
    x = x_ref[...]
    ms = jnp.mean(x * x, axis=-1, keepdims=True)
    h = (x * lax.rsqrt(ms + RMS_EPS) * g_ref[...]).astype(BF16)
    h_ref[...] = h
    tn = w_refs[0].shape[1]
    for k, w_ref in enumerate(w_refs):
        z_ref[:, k * tn:(k + 1) * tn] = jnp.dot(h, w_ref[...].astype(BF16),
                                                preferred_element_type=F32).astype(BF16)


def _proj_norm(x2, g, w, col_start, width, *, tm, tn):
    n, d = x2.shape
    assert n % tm == 0 and width % tn == 0 and col_start % tn == 0
    col_off = col_start // tn
    w_specs = [pl.BlockSpec((d, tn), functools.partial(lambda i, k: (0, k), k=col_off + k),
                            pipeline_mode=pl.Buffered(1)) for k in range(width // tn)]
    return pl.pallas_call(
        _proj_norm_kernel,
        out_shape=(jax.ShapeDtypeStruct((n, width), BF16), jax.ShapeDtypeStruct((n, d), BF16)),
        grid=(n // tm,),
        in_specs=[pl.BlockSpec((tm, d), lambda i: (i, 0)), pl.BlockSpec((1, d), lambda i: (0, 0))] + w_specs,
        out_specs=(pl.BlockSpec((tm, width), lambda i: (i, 0)), pl.BlockSpec((tm, d), lambda i: (i, 0))),
        compiler_params=_cparams(("parallel",)),
        name="proj_vf_norm",
    )(x2, g.reshape(1, d), *([w] * (width // tn)))


def _proj_sigmoid_kernel(h_ref, w_ref, wg_ref, wu_ref, wd_ref, z_ref, wg_out, wu_out, wd_out):
    wg_out[...] = wg_ref[...].astype(BF16)
    wu_out[...] = wu_ref[...].astype(BF16)

    @pl.when(pl.program_id(0) == 0)
    def _():
        wd_out[...] = wd_ref[...].astype(BF16)

    acc = jnp.dot(h_ref[...], w_ref[...].astype(BF16), preferred_element_type=F32)
    z_ref[...] = jax.nn.sigmoid(acc).astype(BF16)


def _proj_gates(h, w, col_start, width, w_gate, w_up, w_down, *, tm, tn):
    n, d = h.shape
    dff = w_gate.shape[1]
    ni, nj = n // tm, width // tn
    assert n % tm == 0 and width % tn == 0 and col_start % tn == 0
    rg, rd = d // (ni * nj), dff // ni
    assert rg * ni * nj == d and rd * ni == dff and rg % 16 == 0 and rd % 16 == 0
    col_off = col_start // tn
    step = lambda j, i: (j * ni + i, 0)
    down = lambda j, i: (jnp.where(j == 0, i, ni - 1), 0)
    return pl.pallas_call(
        _proj_sigmoid_kernel,
        out_shape=(jax.ShapeDtypeStruct((n, width), BF16), jax.ShapeDtypeStruct(w_gate.shape, BF16),
                   jax.ShapeDtypeStruct(w_up.shape, BF16), jax.ShapeDtypeStruct(w_down.shape, BF16)),
        grid=(nj, ni),
        in_specs=[pl.BlockSpec((tm, d), lambda j, i: (i, 0)), pl.BlockSpec((d, tn), lambda j, i: (0, j + col_off)),
                  pl.BlockSpec((rg, dff), step), pl.BlockSpec((rg, dff), step), pl.BlockSpec((rd, d), down)],
        out_specs=(pl.BlockSpec((tm, tn), lambda j, i: (i, j)), pl.BlockSpec((rg, dff), step),
                   pl.BlockSpec((rg, dff), step), pl.BlockSpec((rd, d), down)),
        compiler_params=_cparams(("arbitrary", "arbitrary")),
        name="proj_gates",
    )(h, w, w_gate, w_up, w_down)


def _proj_rope(h, w, col_start, width, rope, *, seq, q_width, tm, tn):
    n, d = h.shape
    assert n % tm == 0 and width % tn == 0 and col_start % tn == 0
    assert seq % tm == 0 and q_width % tn == 0 and tn % HEAD_DIM == 0
    col_off = col_start // tn
    s_tiles, q_tiles = seq // tm, q_width // tn
    table = pl.BlockSpec((1, tm, HEAD_DIM), lambda j, i: (j // q_tiles, i % s_tiles, 0))
    return pl.pallas_call(
        _proj_rope_kernel,
        out_shape=jax.ShapeDtypeStruct((n, width), BF16),
        grid=(width // tn, n // tm),
        in_specs=[pl.BlockSpec((tm, d), lambda j, i: (i, 0)), pl.BlockSpec((d, tn), lambda j, i: (0, j + col_off)),
                  table, table, table],
        out_specs=pl.BlockSpec((tm, tn), lambda j, i: (i, j)),
        compiler_params=_cparams(("arbitrary", "arbitrary")),
        name="proj_qk_rope",
    )(h, w, *rope)


_NT = (((1,), (1,)), ((), ()))


def _lane_fold(x, op, ways=4):
    n = x.shape[1] // V7X_LANES
    ways = min(ways, n)
    parts = [x[:, i * V7X_LANES:(i + 1) * V7X_LANES] for i in range(ways)]
    for i in range(ways, n):
        parts[i % ways] = op(parts[i % ways], x[:, i * V7X_LANES:(i + 1) * V7X_LANES])
    while len(parts) > 1:
        parts = [op(parts[2 * i], parts[2 * i + 1]) for i in range(len(parts) // 2)] + parts[len(parts) // 2 * 2:]
    return parts[0]


def _tile_rows(tile, tq):
    if isinstance(tile, int):
        return slice(tile * tq, (tile + 1) * tq)
    return pl.ds(pl.multiple_of(tile * tq, tq), tq)


def _attn_kernel(lamv_ref, g_ref, q_ref, k_ref, v_ref, o_ref, s_ref, p_ref, m_ref, mb_ref, l_ref, r_ref, il_ref,
                 *, tq, n_steps):
    seq = k_ref.shape[1]
    n_tiles = seq // tq
    tk = seq // n_steps
    tr = tq // n_steps
    lv = lamv_ref[...]
    lam = (jnp.exp(jnp.sum(lv[0:1] * lv[1:2], axis=-1, keepdims=True))
           - jnp.exp(jnp.sum(lv[2:3] * lv[3:4], axis=-1, keepdims=True)) + LAMBDA_INIT)
    gain = g_ref[...] * (1.0 - LAMBDA_INIT)

    def qk_piece(tile, slot, j):
        rows = _tile_rows(tile, tq)
        cols = slice(j * tk, (j + 1) * tk)
        for c in range(2):
            qc = q_ref[0, rows, c * HEAD_DIM:(c + 1) * HEAD_DIM]
            kc = k_ref[0, cols, c * HEAD_DIM:(c + 1) * HEAD_DIM]
            s = lax.dot_general(qc, kc, _NT, preferred_element_type=F32)
            s_ref[slot, c, :, cols] = s
            mx = _lane_fold(s, jnp.maximum)
            m_ref[slot, c] = mx if j == 0 else jnp.maximum(m_ref[slot, c], mx)

    def exp_prep(slot):
        for c in range(2):
            m = jnp.max(m_ref[slot, c], axis=-1, keepdims=True)
            mb_ref[c] = jnp.broadcast_to(m, (tq, V7X_LANES))

    def exp_piece(slot, i, pace):
        rs = slice(i * tr, (i + 1) * tr)
        for c in range(2):
            mrow = mb_ref[c, rs, :]
            if pace is not None:
                mrow = mrow + 0.0 * jnp.concatenate([pace] * (tr // V7X_SUBLANES), axis=0)
            m = jnp.concatenate([mrow] * (seq // V7X_LANES), axis=1)
            p = jnp.exp2(s_ref[slot, c, rs, :] - m)
            p_ref[slot, c, rs, :] = p.astype(BF16)
            l_ref[slot, c, rs, :] = _lane_fold(p, jnp.add)

    def pv_prep(slot):
        l1 = jnp.sum(l_ref[slot, 0], axis=-1, keepdims=True)
        l2 = jnp.sum(l_ref[slot, 1], axis=-1, keepdims=True)
        r_ref[...] = jnp.broadcast_to(lam * l1 / l2, (tq, V7X_LANES)).astype(BF16)
        il_ref[...] = jnp.broadcast_to(1.0 / l1, (tq, V7X_LANES))

    def pv_piece(slot, i, acc):
        cols = slice(i * tk, (i + 1) * tk)
        r = jnp.concatenate([r_ref[...]] * (tk // V7X_LANES), axis=1)
        a = p_ref[slot, 0, :, cols] - r * p_ref[slot, 1, :, cols]
        d = jnp.dot(a, v_ref[0, cols, :], preferred_element_type=F32)
        return d if acc is None else acc + d

    def pv_finish(tile, acc):
        o = acc * jnp.concatenate([il_ref[...]] * (acc.shape[1] // V7X_LANES), axis=1)
        ms = jnp.mean(o * o, axis=-1, keepdims=True)
        o_ref[0, _tile_rows(tile, tq), :] = (o * lax.rsqrt(ms + RMS_EPS) * gain).astype(BF16)

    def period(p, par, do_qk, do_exp, do_pv):
        if do_exp:
            exp_prep(1 - par)
        if do_pv:
            pv_prep(par)
        acc = None
        pace = None
        for i in range(n_steps):
            if do_qk:
                qk_piece(p, par, i)
            if do_exp:
                exp_piece(1 - par, i, pace)
            if do_pv:
                acc = pv_piece(par, i, acc)
                pace = acc[0:V7X_SUBLANES, 0:V7X_LANES]
        if do_pv:
            pv_finish(p - 2, acc)

    period(0, 0, True, False, False)
    period(1, 1, True, True, False)

    def body(g, carry):
        period(2 * g + 2, 0, True, True, True)
        period(2 * g + 3, 1, True, True, True)
        return carry

    lax.fori_loop(0, (n_tiles - 2) // 2, body, 0)
    period(n_tiles, 0, False, True, True)
    period(n_tiles + 1, 1, False, False, True)


def _attention(qk3, vf3, lamv, g_subln, *, n_heads, tq, n_steps):
    b, s, _ = qk3.shape
    vd = 2 * HEAD_DIM
    assert s % (2 * tq) == 0 and (s // n_steps) % V7X_LANES == 0 and (tq // n_steps) % 16 == 0
    return pl.pallas_call(
        functools.partial(_attn_kernel, tq=tq, n_steps=n_steps),
        out_shape=jax.ShapeDtypeStruct((b, s, n_heads * vd), BF16),
        grid=(b, n_heads),
        in_specs=[
            pl.BlockSpec((4, HEAD_DIM), lambda bi, h: (0, 0)),
            pl.BlockSpec((1, vd), lambda bi, h: (0, 0)),
            pl.BlockSpec((1, s, vd), lambda bi, h: (bi, 0, h)),
            pl.BlockSpec((1, s, vd), lambda bi, h: (bi, 0, n_heads + h)),
            pl.BlockSpec((1, s, vd), lambda bi, h: (bi, 0, h)),
        ],
        out_specs=pl.BlockSpec((1, s, vd), lambda bi, h: (bi, 0, h)),
        scratch_shapes=[
            pltpu.VMEM((2, 2, tq, s), F32),
            pltpu.VMEM((2, 2, tq, s), BF16),
            pltpu.VMEM((2, 2, tq, V7X_LANES), F32),
            pltpu.VMEM((2, tq, V7X_LANES), F32),
            pltpu.VMEM((2, 2, tq, V7X_LANES), F32),
            pltpu.VMEM((tq, V7X_LANES), BF16),
            pltpu.VMEM((tq, V7X_LANES), F32),
        ],
        compiler_params=_cparams(("parallel", "parallel")),
        name="diff_attn",
    )(lamv, g_subln.reshape(1, vd), qk3, qk3, vf3)


def _cmul_const(x, wr, wi):
    xr, xi = x
    wr = 0.0 if abs(wr) < 1e-12 else wr
    wi = 0.0 if abs(wi) < 1e-12 else wi
    if wi == 0.0:
        return x if wr == 1.0 else ((-xr, -xi) if wr == -1.0 else (xr * wr, xi * wr))
    if wr == 0.0:
        return (xi, -xr) if wi == -1.0 else ((-xi, xr) if wi == 1.0 else (-xi * wi, xr * wi))
    return (xr * wr - xi * wi, xr * wi + xi * wr)


def _fft_list(xs):
    n = len(xs)
    if n == 1:
        return xs
    ev = _fft_list(xs[0::2])
    od = _fft_list(xs[1::2])
    out = [None] * n
    for k in range(n // 2):
        ang = -2.0 * math.pi * k / n
        tr, ti = _cmul_const(od[k], math.cos(ang), math.sin(ang))
        out[k] = (ev[k][0] + tr, ev[k][1] + ti)
        out[k + n // 2] = (ev[k][0] - tr, ev[k][1] - ti)
    return out


def _fourier_kernel(f_ref, cd_ref, twc_ref, tws_ref, cs_ref, out_ref, g_ref):
    _, n1, n2, c = f_ref.shape
    n_groups = c // FOURIER_GROUP_DIM
    f2 = f_ref[0].reshape(n1 * n2, c)
    for grp in range(n_groups):
        lo = grp * FOURIER_GROUP_DIM
        gg = jnp.dot(f2[:, lo:lo + FOURIER_GROUP_DIM], cd_ref[...], preferred_element_type=F32)
        g_ref[:, :, lo:lo + FOURIER_GROUP_DIM] = gg[:, :FOURIER_GROUP_DIM].reshape(n1, n2, FOURIER_GROUP_DIM)
        g_ref[:, :, c + lo:c + lo + FOURIER_GROUP_DIM] = gg[:, FOURIER_GROUP_DIM:].reshape(n1, n2, FOURIER_GROUP_DIM)

    def body(t, carry):
        r = pl.multiple_of(t * V7X_SUBLANES, V7X_SUBLANES)
        rows = pl.ds(r, V7X_SUBLANES)
        for lc in range(c // V7X_LANES):
            re_l = slice(lc * V7X_LANES, (lc + 1) * V7X_LANES)
            im_l = slice(c + lc * V7X_LANES, c + (lc + 1) * V7X_LANES)
            ys = _fft_list([(g_ref[a, rows, re_l], g_ref[a, rows, im_l]) for a in range(n1)])
            for k1 in range(n1):
                yr, yi = ys[k1]
                tc = twc_ref[k1, rows, :]
                ts = tws_ref[k1, rows, :]
                g_ref[k1, rows, re_l] = yr * tc + yi * ts
                g_ref[k1, rows, im_l] = yi * tc - yr * ts
        return carry

    lax.fori_loop(0, n2 // V7X_SUBLANES, body, 0)

    cc = cs_ref[:, :n2]
    ss = cs_ref[:, n2:]
    for k1 in range(n1):
        zr = g_ref[k1, :, :c].astype(BF16)
        zi = g_ref[k1, :, c:].astype(BF16)
        xk = (jnp.dot(cc, zr, preferred_element_type=F32) + jnp.dot(ss, zi, preferred_element_type=F32))
        out_ref[0, :, k1 * c:(k1 + 1) * c] = xk.astype(BF16)


def _fourier(z4, cd, twc, tws, cs, *, f_col_block, c):
    b, n1, n2, _ = z4.shape
    return pl.pallas_call(
        _fourier_kernel,
        out_shape=jax.ShapeDtypeStruct((b, n2, n1 * c), BF16),
        grid=(b,),
        in_specs=[
            pl.BlockSpec((1, n1, n2, c), lambda bi: (bi, 0, 0, f_col_block)),
            pl.BlockSpec(cd.shape, lambda bi: (0, 0)),
            pl.BlockSpec(twc.shape, lambda bi: (0, 0, 0)),
            pl.BlockSpec(tws.shape, lambda bi: (0, 0, 0)),
            pl.BlockSpec(cs.shape, lambda bi: (0, 0)),
        ],
        out_specs=pl.BlockSpec((1, n2, n1 * c), lambda bi: (bi, 0, 0)),
        scratch_shapes=[pltpu.VMEM((n1, n2, 2 * c), F32)],
        compiler_params=_cparams(("parallel",)),
        name="fourier_mix",
    )(z4, cd, twc, tws, cs)


def _fourier_tables(seq):
    n1, n2 = FFT_RADIX, seq // FFT_RADIX
    gd = FOURIER_GROUP_DIM

    def cos_sin(n, scale):
        idx = jnp.arange(n, dtype=jnp.int32)
        ang = ((idx[:, None] * idx[None, :]) % n).astype(F32) * (2.0 * math.pi / n)
        return jnp.cos(ang) * scale, jnp.sin(ang) * scale

    cg, sg = cos_sin(gd, gd ** -0.5)
    cd = jnp.concatenate([cg, -sg], axis=1).astype(BF16)
    c2, s2 = cos_sin(n2, seq ** -0.5)
    cs = jnp.concatenate([c2, s2], axis=1).astype(BF16)
    k1 = jnp.arange(n1, dtype=jnp.int32)[:, None]
    r = jnp.arange(n2, dtype=jnp.int32)[None, :]
    ang = (k1 * r).astype(F32) * (2.0 * math.pi / seq)
    twc = jnp.broadcast_to(jnp.cos(ang)[:, :, None], (n1, n2, V7X_LANES))
    tws = jnp.broadcast_to(jnp.sin(ang)[:, :, None], (n1, n2, V7X_LANES))
    return cd, twc, tws, cs


def _merge_kernel(o_ref, fm_ref, ga0_ref, ga1_ref, gf0_ref, gf1_ref, x_ref, wa_ref, wf_ref, wo_ref,
                  g_ref, x1_ref, h2_ref):
    o = o_ref[...]
    fm = fm_ref[...]
    acc = x_ref[...]
    half = ga0_ref.shape[1]
    for ci, (ga_ref, gf_ref) in enumerate(((ga0_ref, gf0_ref), (ga1_ref, gf1_ref))):
        cols = slice(ci * half, (ci + 1) * half)
        br_a = jnp.dot(o, wa_ref[:, cols], preferred_element_type=F32)
        br_f = jnp.dot(fm, wf_ref[:, cols], preferred_element_type=F32)
        m = ga_ref[...].astype(F32) * br_a + gf_ref[...].astype(F32) * br_f
        acc = acc + jnp.dot(m.astype(BF16), wo_ref[cols, :], preferred_element_type=F32)
    x1_ref[...] = acc
    ms = jnp.mean(acc * acc, axis=-1, keepdims=True)
    h2_ref[...] = (acc * lax.rsqrt(ms + RMS_EPS) * g_ref[...]).astype(BF16)


def _merge(o2, fm2, gates, x2, wa, wf, wo, g_ffn, *, tm):
    n, d = x2.shape
    half = d // 2
    assert gates.shape[1] == 2 * d and n % tm == 0
    const = lambda i: (0, 0)
    return pl.pallas_call(
        _merge_kernel,
        out_shape=(jax.ShapeDtypeStruct((n, d), F32), jax.ShapeDtypeStruct((n, d), BF16)),
        grid=(n // tm,),
        in_specs=[
            pl.BlockSpec((tm, o2.shape[1]), lambda i: (i, 0)),
            pl.BlockSpec((tm, fm2.shape[1]), lambda i: (i, 0)),
            pl.BlockSpec((tm, half), lambda i: (i, 0)),
            pl.BlockSpec((tm, half), lambda i: (i, 1)),
            pl.BlockSpec((tm, half), lambda i: (i, 2)),
            pl.BlockSpec((tm, half), lambda i: (i, 3)),
            pl.BlockSpec((tm, d), lambda i: (i, 0)),
            pl.BlockSpec(wa.shape, const, pipeline_mode=pl.Buffered(1)),
            pl.BlockSpec(wf.shape, const, pipeline_mode=pl.Buffered(1)),
            pl.BlockSpec(wo.shape, const, pipeline_mode=pl.Buffered(1)),
            pl.BlockSpec((1, d), const),
        ],
        out_specs=(pl.BlockSpec((tm, d), lambda i: (i, 0)), pl.BlockSpec((tm, d), lambda i: (i, 0))),
        compiler_params=_cparams(("parallel",)),
        name="merge_out_proj",
    )(o2, fm2, gates, gates, gates, gates, x2, wa, wf, wo, g_ffn.reshape(1, d))


def _ffn_kernel(h_ref, x1_ref, wg_ref, wu_ref, wd_ref, g_ref, out_ref):
    j = pl.program_id(1)

    @pl.when(j == 0)
    def _():
        out_ref[...] = x1_ref[...]

    h = h_ref[...]
    gate = jnp.dot(h, wg_ref[...], preferred_element_type=F32)
    up = jnp.dot(h, wu_ref[...], preferred_element_type=F32)
    act = (gate * jax.nn.sigmoid(gate) * up).astype(BF16)
    out_ref[...] += jnp.dot(act, wd_ref[...], preferred_element_type=F32)

    @pl.when(j == pl.num_programs(1) - 1)
    def _():
        y = out_ref[...]
        ms = jnp.mean(y * y, axis=-1, keepdims=True)
        out_ref[...] = y * lax.rsqrt(ms + RMS_EPS) * g_ref[...]


def _ffn(h2, x1, wg, wu, wd, g_final, *, tm, tf):
    n, d = x1.shape
    dff = wg.shape[1]
    assert n % tm == 0 and dff % tf == 0
    return pl.pallas_call(
        _ffn_kernel,
        out_shape=jax.ShapeDtypeStruct((n, d), F32),
        grid=(n // tm, dff // tf),
        in_specs=[
            pl.BlockSpec((tm, d), lambda i, j: (i, 0)),
            pl.BlockSpec((tm, d), lambda i, j: (i, 0)),
            pl.BlockSpec((d, tf), lambda i, j: (0, j)),
            pl.BlockSpec((d, tf), lambda i, j: (0, j)),
            pl.BlockSpec((tf, d), lambda i, j: (j, 0)),
            pl.BlockSpec((1, d), lambda i, j: (0, 0)),
        ],
        out_specs=pl.BlockSpec((tm, d), lambda i, j: (i, 0)),
        compiler_params=_cparams(("parallel", "arbitrary")),
        name="swiglu_ffn",
    )(h2, x1, wg, wu, wd, g_final.reshape(1, d))


def _rope_tables(seq):
    half = ROT_DIM // 2
    pos = jnp.arange(seq, dtype=F32)
    inv_freq = ROPE_THETA ** (-jnp.arange(0, ROT_DIM, 2, dtype=F32) / ROT_DIM)
    ang = pos[:, None] * inv_freq[None, :]
    cos, sin = jnp.cos(ang), jnp.sin(ang)
    ones = jnp.ones((seq, HEAD_DIM - ROT_DIM), F32)
    zeros = lambda w: jnp.zeros((seq, w), F32)
    rope_c = jnp.concatenate([cos, cos, ones], axis=1)
    rope_sa = jnp.concatenate([-sin, zeros(HEAD_DIM - half)], axis=1)
    rope_sb = jnp.concatenate([zeros(half), sin, zeros(HEAD_DIM - ROT_DIM)], axis=1)
    q_scale = math.log2(math.e) / math.sqrt(HEAD_DIM)
    return tuple(jnp.stack([t * q_scale, t]) for t in (rope_c, rope_sa, rope_sb))


def _tile(n, pref):
    t = min(n, pref)
    while n % t:
        t //= 2
    return t


def kernel(x, g_mix, w_in, lambda_q1, lambda_k1, lambda_q2, lambda_k2, g_subln, w_attn_branch,
           w_four_branch, w_out, g_ffn, w_gate, w_up, w_down, g_final):
    b, s, d = x.shape
    n = b * s
    v_width = w_attn_branch.shape[0]
    f_width = w_four_branch.shape[0]
    vd = g_subln.shape[0]
    n_heads = v_width // vd
    qk_width = n_heads * 2 * HEAD_DIM
    f_start = 2 * qk_width + v_width
    gate_start = f_start + f_width
    assert vd == 2 * HEAD_DIM and w_in.shape[1] == gate_start + 2 * d
    assert v_width % f_width == 0 and s % (FFT_RADIX * 2 * V7X_SUBLANES) == 0

    x2 = x.reshape(n, d)
    tm = _tile(s, 1024)
    vf, h = _proj_norm(x2, g_mix, w_in, 2 * qk_width, v_width + f_width, tm=_tile(n, 512), tn=1024)
    qk = _proj_rope(h, w_in, 0, 2 * qk_width, _rope_tables(s), seq=s, q_width=qk_width, tm=tm, tn=qk_width)
    gates, wg_bf, wu_bf, wd_bf = _proj_gates(h, w_in, gate_start, 2 * d, w_gate, w_up, w_down, tm=tm, tn=1024)

    lamv = jnp.stack([lambda_q1, lambda_k1, lambda_q2, lambda_k2]).astype(F32)
    o = _attention(qk.reshape(b, s, -1), vf.reshape(b, s, -1), lamv, g_subln, n_heads=n_heads, tq=256,
                   n_steps=min(16, s // 256))

    cd, twc, tws, cs = _fourier_tables(s)
    fm = _fourier(vf.reshape(b, FFT_RADIX, s // FFT_RADIX, -1), cd, twc, tws, cs,
                  f_col_block=v_width // f_width, c=f_width)

    x1, h2 = _merge(o.reshape(n, v_width), fm.reshape(n, f_width), gates, x2, w_attn_branch.astype(BF16),
                    w_four_branch.astype(BF16), w_out.astype(BF16), g_ffn, tm=_tile(n, 512))

    out = _ffn(h2, x1, wg_bf, wu_bf, wd_bf, g_final, tm=_tile(n, 1024), tf=512)
    return out.reshape(b, s, d)
```

```python
import functools
import math

import jax
import jax.numpy as jnp
from jax import lax
from jax.experimental import pallas as pl
from jax.experimental.pallas import tpu as pltpu

F32 = jnp.float32
BF16 = jnp.bfloat16

HEAD_DIM = 128
ROT_DIM = HEAD_DIM // 4
ROPE_THETA = 500000.0
LAMBDA_INIT = 0.8 - 0.6 * math.exp(-0.3 * 0)
FOURIER_GROUP_DIM = 128
RMS_EPS = 1e-5
FFT_RADIX = 16

V7X_LANES = 128
V7X_SUBLANES = 8
V7X_VMEM_LIMIT_BYTES = 58 * 1024 * 1024


def _cparams(sem):
    return pltpu.CompilerParams(dimension_semantics=sem, vmem_limit_bytes=V7X_VMEM_LIMIT_BYTES)


def _proj_rope_kernel(h_ref, w_ref, cos_ref, sa_ref, sb_ref, *refs):
    n_side = len(refs) // 2
    z_ref = refs[n_side]
    for src, dst in zip(refs[:n_side], refs[n_side + 1:]):
        dst[...] = src[...].astype(BF16)

    acc = jnp.dot(h_ref[...], w_ref[...].astype(BF16), preferred_element_type=F32)
    tn = acc.shape[1]
    rep = tn // HEAD_DIM
    c = jnp.concatenate([cos_ref[0]] * rep, axis=1)
    sa = jnp.concatenate([sa_ref[0]] * rep, axis=1)
    sb = jnp.concatenate([sb_ref[0]] * rep, axis=1)
    half = ROT_DIM // 2
    r = acc * c + pltpu.roll(acc, tn - half, 1) * sa + pltpu.roll(acc, half, 1) * sb
    z_ref[...] = r.astype(BF16)


def _proj_norm_kernel(x_ref, g_ref, *refs):
    w_refs, z_ref, h_ref = refs[:-2], refs[-2], refs[-1]
    x = x_ref[...]
    ms = jnp.mean(x * x, axis=-1, keepdims=True)
    h = (x * lax.rsqrt(ms + RMS_EPS) * g_ref[...]).astype(BF16)
    h_ref[...] = h
    tn = w_refs[0].shape[1]
    for k, w_ref in enumerate(w_refs):
        z_ref[:, k * tn:(k + 1) * tn] = jnp.dot(h, w_ref[...].astype(BF16),
                                                preferred_element_type=F32).astype(BF16)


def _proj_norm(x2, g, w, col_start, width, *, tm, tn):
    n, d = x2.shape
    assert n % tm == 0 and width % tn == 0 and col_start % tn == 0
    col_off = col_start // tn
    w_specs = [pl.BlockSpec((d, tn), functools.partial(lambda i, k: (0, k), k=col_off + k),
                            pipeline_mode=pl.Buffered(1)) for k in range(width // tn)]
    return pl.pallas_call(
        _proj_norm_kernel,
        out_shape=(jax.ShapeDtypeStruct((n, width), BF16), jax.ShapeDtypeStruct((n, d), BF16)),
        grid=(n // tm,),
        in_specs=[pl.BlockSpec((tm, d), lambda i: (i, 0)), pl.BlockSpec((1, d), lambda i: (0, 0))] + w_specs,
        out_specs=(pl.BlockSpec((tm, width), lambda i: (i, 0)), pl.BlockSpec((tm, d), lambda i: (i, 0))),
        compiler_params=_cparams(("parallel",)),
        name="proj_vf_norm",
    )(x2, g.reshape(1, d), *([w] * (width // tn)))


def _proj_sigmoid_kernel(h_ref, w_ref, wg_ref, wu_ref, wd_ref, z_ref, wg_out, wu_out, wd_out):
    wg_out[...] = wg_ref[...].astype(BF16)
    wu_out[...] = wu_ref[...].astype(BF16)

    @pl.when(pl.program_id(0) == 0)
    def _():
        wd_out[...] = wd_ref[...].astype(BF16)

    acc = jnp.dot(h_ref[...], w_ref[...].astype(BF16), preferred_element_type=F32)
    z_ref[...] = jax.nn.sigmoid(acc).astype(BF16)


def _proj_gates(h, w, col_start, width, w_gate, w_up, w_down, *, tm, tn):
    n, d = h.shape
    dff = w_gate.shape[1]
    ni, nj = n // tm, width // tn
    assert n % tm == 0 and width % tn == 0 and col_start % tn == 0
    rg, rd = d // (ni * nj), dff // ni
    assert rg * ni * nj == d and rd * ni == dff and rg % 16 == 0 and rd % 16 == 0
    col_off = col_start // tn
    step = lambda j, i: (j * ni + i, 0)
    down = lambda j, i: (jnp.where(j == 0, i, ni - 1), 0)
    return pl.pallas_call(
        _proj_sigmoid_kernel,
        out_shape=(jax.ShapeDtypeStruct((n, width), BF16), jax.ShapeDtypeStruct(w_gate.shape, BF16),
                   jax.ShapeDtypeStruct(w_up.shape, BF16), jax.ShapeDtypeStruct(w_down.shape, BF16)),
        grid=(nj, ni),
        in_specs=[pl.BlockSpec((tm, d), lambda j, i: (i, 0)), pl.BlockSpec((d, tn), lambda j, i: (0, j + col_off)),
                  pl.BlockSpec((rg, dff), step), pl.BlockSpec((rg, dff), step), pl.BlockSpec((rd, d), down)],
        out_specs=(pl.BlockSpec((tm, tn), lambda j, i: (i, j)), pl.BlockSpec((rg, dff), step),
                   pl.BlockSpec((rg, dff), step), pl.BlockSpec((rd, d), down)),
        compiler_params=_cparams(("arbitrary", "arbitrary")),
        name="proj_gates",
    )(h, w, w_gate, w_up, w_down)


def _proj_rope(h, w, col_start, width, rope, side_weights, *, seq, q_width, tm, tn):
    n, d = h.shape
    assert n % tm == 0 and width % tn == 0 and col_start % tn == 0
    assert seq % tm == 0 and q_width % tn == 0 and tn % HEAD_DIM == 0
    col_off = col_start // tn
    s_tiles, q_tiles = seq // tm, q_width // tn
    table = pl.BlockSpec((1, tm, HEAD_DIM), lambda j, i: (j // q_tiles, i % s_tiles, 0))
    nj, ni = width // tn, n // tm
    side_specs = []
    for sw in side_weights:
        rows = sw.shape[0] // (nj * ni)
        assert rows * nj * ni == sw.shape[0] and rows % 16 == 0
        side_specs.append(pl.BlockSpec((rows, sw.shape[1]), lambda j, i: (j * ni + i, 0)))
    return pl.pallas_call(
        _proj_rope_kernel,
        out_shape=(jax.ShapeDtypeStruct((n, width), BF16),
                   *[jax.ShapeDtypeStruct(sw.shape, BF16) for sw in side_weights]),
        grid=(nj, ni),
        in_specs=[pl.BlockSpec((tm, d), lambda j, i: (i, 0)), pl.BlockSpec((d, tn), lambda j, i: (0, j + col_off)),
                  table, table, table] + side_specs,
        out_specs=(pl.BlockSpec((tm, tn), lambda j, i: (i, j)), *side_specs),
        compiler_params=_cparams(("arbitrary", "arbitrary")),
        name="proj_qk_rope",
    )(h, w, *rope, *side_weights)


_NT = (((1,), (1,)), ((), ()))


def _lane_fold(x, op, ways=4):
    n = x.shape[1] // V7X_LANES
    ways = min(ways, n)
    parts = [x[:, i * V7X_LANES:(i + 1) * V7X_LANES] for i in range(ways)]
    for i in range(ways, n):
        parts[i % ways] = op(parts[i % ways], x[:, i * V7X_LANES:(i + 1) * V7X_LANES])
    while len(parts) > 1:
        parts = [op(parts[2 * i], parts[2 * i + 1]) for i in range(len(parts) // 2)] + parts[len(parts) // 2 * 2:]
    return parts[0]


def _tile_rows(tile, tq):
    if isinstance(tile, int):
        return slice(tile * tq, (tile + 1) * tq)
    return pl.ds(pl.multiple_of(tile * tq, tq), tq)


def _attn_kernel(lamv_ref, g_ref, q_ref, k_ref, v_ref, o_ref, s_ref, p_ref, m_ref, mb_ref, l_ref, r_ref, il_ref,
                 *, tq, n_steps):
    seq = k_ref.shape[1]
    n_tiles = seq // tq
    tk = seq // n_steps
    tr = tq // n_steps
    lv = lamv_ref[...]
    lam = (jnp.exp(jnp.sum(lv[0:1] * lv[1:2], axis=-1, keepdims=True))
           - jnp.exp(jnp.sum(lv[2:3] * lv[3:4], axis=-1, keepdims=True)) + LAMBDA_INIT)
    gain = g_ref[...] * (1.0 - LAMBDA_INIT)

    def qk_piece(tile, slot, j):
        rows = _tile_rows(tile, tq)
        cols = slice(j * tk, (j + 1) * tk)
        for c in range(2):
            qc = q_ref[0, rows, c * HEAD_DIM:(c + 1) * HEAD_DIM]
            kc = k_ref[0, cols, c * HEAD_DIM:(c + 1) * HEAD_DIM]
            s = lax.dot_general(qc, kc, _NT, preferred_element_type=F32)
            s_ref[slot, c, :, cols] = s
            mx = _lane_fold(s, jnp.maximum)
            m_ref[slot, c] = mx if j == 0 else jnp.maximum(m_ref[slot, c], mx)

    def exp_prep(slot):
        for c in range(2):
            m = jnp.max(m_ref[slot, c], axis=-1, keepdims=True)
            mb_ref[c] = jnp.broadcast_to(m, (tq, V7X_LANES))

    def exp_piece(slot, i, pace):
        rs = slice(i * tr, (i + 1) * tr)
        for c in range(2):
            mrow = mb_ref[c, rs, :]
            if pace is not None:
                mrow = mrow + 0.0 * jnp.concatenate([pace] * (tr // V7X_SUBLANES), axis=0)
            m = jnp.concatenate([mrow] * (seq // V7X_LANES), axis=1)
            p = jnp.exp2(s_ref[slot, c, rs, :] - m)
            p_ref[slot, c, rs, :] = p.astype(BF16)
            l_ref[slot, c, rs, :] = _lane_fold(p, jnp.add)

    def pv_prep(slot):
        l1 = jnp.sum(l_ref[slot, 0], axis=-1, keepdims=True)
        l2 = jnp.sum(l_ref[slot, 1], axis=-1, keepdims=True)
        r_ref[...] = jnp.broadcast_to(lam * l1 / l2, (tq, V7X_LANES)).astype(BF16)
        il_ref[...] = jnp.broadcast_to(1.0 / l1, (tq, V7X_LANES))

    def pv_piece(slot, i, acc):
        cols = slice(i * tk, (i + 1) * tk)
        r = jnp.concatenate([r_ref[...]] * (tk // V7X_LANES), axis=1)
        a = p_ref[slot, 0, :, cols] - r * p_ref[slot, 1, :, cols]
        d = jnp.dot(a, v_ref[0, cols, :], preferred_element_type=F32)
        return d if acc is None else acc + d

    def pv_finish(tile, acc):
        o = acc * jnp.concatenate([il_ref[...]] * (acc.shape[1] // V7X_LANES), axis=1)
        ms = jnp.mean(o * o, axis=-1, keepdims=True)
        o_ref[0, _tile_rows(tile, tq), :] = (o * lax.rsqrt(ms + RMS_EPS) * gain).astype(BF16)

    def period(p, par, do_qk, do_exp, do_pv):
        if do_exp:
            exp_prep(1 - par)
        if do_pv:
            pv_prep(par)
        acc = None
        pace = None
        for i in range(n_steps):
            if do_qk:
                qk_piece(p, par, i)
            if do_exp:
                exp_piece(1 - par, i, pace)
            if do_pv:
                acc = pv_piece(par, i, acc)
                pace = acc[0:V7X_SUBLANES, 0:V7X_LANES]
        if do_pv:
            pv_finish(p - 2, acc)

    period(0, 0, True, False, False)
    period(1, 1, True, True, False)

    def body(g, carry):
        period(2 * g + 2, 0, True, True, True)
        period(2 * g + 3, 1, True, True, True)
        return carry

    lax.fori_loop(0, (n_tiles - 2) // 2, body, 0)
    period(n_tiles, 0, False, True, True)
    period(n_tiles + 1, 1, False, False, True)


def _attention(qk3, vf3, lamv, g_subln, *, n_heads, tq, n_steps):
    b, s, _ = qk3.shape
    vd = 2 * HEAD_DIM
    assert s % (2 * tq) == 0 and (s // n_steps) % V7X_LANES == 0 and (tq // n_steps) % 16 == 0
    return pl.pallas_call(
        functools.partial(_attn_kernel, tq=tq, n_steps=n_steps),
        out_shape=jax.ShapeDtypeStruct((b, s, n_heads * vd), BF16),
        grid=(b, n_heads),
        in_specs=[
            pl.BlockSpec((4, HEAD_DIM), lambda bi, h: (0, 0)),
            pl.BlockSpec((1, vd), lambda bi, h: (0, 0)),
            pl.BlockSpec((1, s, vd), lambda bi, h: (bi, 0, h)),
            pl.BlockSpec((1, s, vd), lambda bi, h: (bi, 0, n_heads + h)),
            pl.BlockSpec((1, s, vd), lambda bi, h: (bi, 0, h)),
        ],
        out_specs=pl.BlockSpec((1, s, vd), lambda bi, h: (bi, 0, h)),
        scratch_shapes=[
            pltpu.VMEM((2, 2, tq, s), F32),
            pltpu.VMEM((2, 2, tq, s), BF16),
            pltpu.VMEM((2, 2, tq, V7X_LANES), F32),
            pltpu.VMEM((2, tq, V7X_LANES), F32),
            pltpu.VMEM((2, 2, tq, V7X_LANES), F32),
            pltpu.VMEM((tq, V7X_LANES), BF16),
            pltpu.VMEM((tq, V7X_LANES), F32),
        ],
        compiler_params=_cparams(("parallel", "parallel")),
        name="diff_attn",
    )(lamv, g_subln.reshape(1, vd), qk3, qk3, vf3)


def _cmul_const(x, wr, wi):
    xr, xi = x
    wr = 0.0 if abs(wr) < 1e-12 else wr
    wi = 0.0 if abs(wi) < 1e-12 else wi
    if wi == 0.0:
        return x if wr == 1.0 else ((-xr, -xi) if wr == -1.0 else (xr * wr, xi * wr))
    if wr == 0.0:
        return (xi, -xr) if wi == -1.0 else ((-xi, xr) if wi == 1.0 else (-xi * wi, xr * wi))
    return (xr * wr - xi * wi, xr * wi + xi * wr)


def _fft_list(xs):
    n = len(xs)
    if n == 1:
        return xs
    ev = _fft_list(xs[0::2])
    od = _fft_list(xs[1::2])
    out = [None] * n
    for k in range(n // 2):
        ang = -2.0 * math.pi * k / n
        tr, ti = _cmul_const(od[k], math.cos(ang), math.sin(ang))
        out[k] = (ev[k][0] + tr, ev[k][1] + ti)
        out[k + n // 2] = (ev[k][0] - tr, ev[k][1] - ti)
    return out


def _fourier_kernel(*refs):
    n1 = FFT_RADIX
    f_refs = refs[:n1]
    ma_ref, twc_ref, tws_ref, cd_ref, out_ref, y_ref = refs[n1:]
    _, n2, c = f_refs[0].shape
    for r in range(n1):
        y = jnp.dot(ma_ref[...], f_refs[r][0], preferred_element_type=F32)
        y_ref[r, :, :c] = y[:n2]
        y_ref[r, :, c:] = y[n2:]

    def body(t, carry):
        rows = pl.ds(pl.multiple_of(t * V7X_SUBLANES, V7X_SUBLANES), V7X_SUBLANES)
        for lc in range(c // V7X_LANES):
            re_l = slice(lc * V7X_LANES, (lc + 1) * V7X_LANES)
            im_l = slice(c + lc * V7X_LANES, c + (lc + 1) * V7X_LANES)
            zs = []
            for r in range(n1):
                yr, yi = y_ref[r, rows, re_l], y_ref[r, rows, im_l]
                tc, ts = twc_ref[r, rows, :], tws_ref[r, rows, :]
                zs.append((yr * tc + yi * ts, yi * tc - yr * ts))
            for k2, (xr, xi) in enumerate(_fft_list(zs)):
                y_ref[k2, rows, re_l] = xr
                y_ref[k2, rows, im_l] = xi
        return carry

    lax.fori_loop(0, n2 // V7X_SUBLANES, body, 0)

    gd = FOURIER_GROUP_DIM
    for k2 in range(n1):
        xk = y_ref[k2].astype(BF16)
        parts = []
        for grp in range(c // gd):
            lhs = jnp.concatenate([xk[:, grp * gd:(grp + 1) * gd], xk[:, c + grp * gd:c + (grp + 1) * gd]], axis=1)
            parts.append(jnp.dot(lhs, cd_ref[...], preferred_element_type=F32))
        out_ref[0, k2 * n2:(k2 + 1) * n2, :] = jnp.concatenate(parts, axis=1).astype(BF16)


def _fourier(vf3, ma, twc, tws, cd, *, f_col_start, c):
    n1 = FFT_RADIX
    b, n2, wide = vf3.shape
    w = wide // n1
    assert f_col_start % c == 0 and w % c == 0
    f_specs = [pl.BlockSpec((1, n2, c), functools.partial(lambda bi, blk: (bi, 0, blk), blk=(r * w + f_col_start) // c))
               for r in range(n1)]
    return pl.pallas_call(
        _fourier_kernel,
        out_shape=jax.ShapeDtypeStruct((b, n1 * n2, c), BF16),
        grid=(b,),
        in_specs=f_specs + [
            pl.BlockSpec(ma.shape, lambda bi: (0, 0)),
            pl.BlockSpec(twc.shape, lambda bi: (0, 0, 0)),
            pl.BlockSpec(tws.shape, lambda bi: (0, 0, 0)),
            pl.BlockSpec(cd.shape, lambda bi: (0, 0)),
        ],
        out_specs=pl.BlockSpec((1, n1 * n2, c), lambda bi: (bi, 0, 0)),
        scratch_shapes=[pltpu.VMEM((n1, n2, 2 * c), F32)],
        compiler_params=_cparams(("parallel",)),
        name="fourier_mix",
    )(*([vf3] * n1), ma, twc, tws, cd)


def _fourier_tables(seq):
    n1, n2 = FFT_RADIX, seq // FFT_RADIX
    gd = FOURIER_GROUP_DIM

    def cos_sin(n, scale):
        idx = jnp.arange(n, dtype=jnp.int32)
        ang = ((idx[:, None] * idx[None, :]) % n).astype(F32) * (2.0 * math.pi / n)
        return jnp.cos(ang) * scale, jnp.sin(ang) * scale

    c2, s2 = cos_sin(n2, seq ** -0.5)
    ma = jnp.concatenate([c2, -s2], axis=0).astype(BF16)
    cg, sg = cos_sin(gd, gd ** -0.5)
    cd = jnp.concatenate([cg, sg], axis=0).astype(BF16)
    r = jnp.arange(n1, dtype=jnp.int32)[:, None]
    k1 = jnp.arange(n2, dtype=jnp.int32)[None, :]
    ang = (r * k1).astype(F32) * (2.0 * math.pi / seq)
    twc = jnp.broadcast_to(jnp.cos(ang)[:, :, None], (n1, n2, V7X_LANES))
    tws = jnp.broadcast_to(jnp.sin(ang)[:, :, None], (n1, n2, V7X_LANES))
    return ma, twc, tws, cd


def _merge_kernel(o_ref, fm_ref, ga0_ref, ga1_ref, gf0_ref, gf1_ref, x_ref, wa_ref, wf_ref, wo_ref,
                  g_ref, x1_ref, h2_ref):
    o = o_ref[...]
    fm = fm_ref[...]
    acc = x_ref[...]
    half = ga0_ref.shape[1]
    for ci, (ga_ref, gf_ref) in enumerate(((ga0_ref, gf0_ref), (ga1_ref, gf1_ref))):
        cols = slice(ci * half, (ci + 1) * half)
        br_a = jnp.dot(o, wa_ref[:, cols], preferred_element_type=F32)
        br_f = jnp.dot(fm, wf_ref[:, cols], preferred_element_type=F32)
        m = ga_ref[...].astype(F32) * br_a + gf_ref[...].astype(F32) * br_f
        acc = acc + jnp.dot(m.astype(BF16), wo_ref[cols, :], preferred_element_type=F32)
    x1_ref[...] = acc
    ms = jnp.mean(acc * acc, axis=-1, keepdims=True)
    h2_ref[...] = (acc * lax.rsqrt(ms + RMS_EPS) * g_ref[...]).astype(BF16)


def _merge(o2, fm2, gates, x2, wa, wf, wo, g_ffn, *, tm):
    n, d = x2.shape
    half = d // 2
    assert gates.shape[1] == 2 * d and n % tm == 0
    const = lambda i: (0, 0)
    return pl.pallas_call(
        _merge_kernel,
        out_shape=(jax.ShapeDtypeStruct((n, d), F32), jax.ShapeDtypeStruct((n, d), BF16)),
        grid=(n // tm,),
        in_specs=[
            pl.BlockSpec((tm, o2.shape[1]), lambda i: (i, 0)),
            pl.BlockSpec((tm, fm2.shape[1]), lambda i: (i, 0)),
            pl.BlockSpec((tm, half), lambda i: (i, 0)),
            pl.BlockSpec((tm, half), lambda i: (i, 1)),
            pl.BlockSpec((tm, half), lambda i: (i, 2)),
            pl.BlockSpec((tm, half), lambda i: (i, 3)),
            pl.BlockSpec((tm, d), lambda i: (i, 0)),
            pl.BlockSpec(wa.shape, const, pipeline_mode=pl.Buffered(1)),
            pl.BlockSpec(wf.shape, const, pipeline_mode=pl.Buffered(1)),
            pl.BlockSpec(wo.shape, const, pipeline_mode=pl.Buffered(1)),
            pl.BlockSpec((1, d), const),
        ],
        out_specs=(pl.BlockSpec((tm, d), lambda i: (i, 0)), pl.BlockSpec((tm, d), lambda i: (i, 0))),
        compiler_params=_cparams(("parallel",)),
        name="merge_out_proj",
    )(o2, fm2, gates, gates, gates, gates, x2, wa, wf, wo, g_ffn.reshape(1, d))


def _ffn_kernel(h_ref, x1_ref, wg_ref, wu_ref, wd_ref, g_ref, out_ref):
    j = pl.program_id(1)

    @pl.when(j == 0)
    def _():
        out_ref[...] = x1_ref[...]

    h = h_ref[...]
    gate = jnp.dot(h, wg_ref[...], preferred_element_type=F32)
    up = jnp.dot(h, wu_ref[...], preferred_element_type=F32)
    act = (gate * jax.nn.sigmoid(gate) * up).astype(BF16)
    out_ref[...] += jnp.dot(act, wd_ref[...], preferred_element_type=F32)

    @pl.when(j == pl.num_programs(1) - 1)
    def _():
        y = out_ref[...]
        ms = jnp.mean(y * y, axis=-1, keepdims=True)
        out_ref[...] = y * lax.rsqrt(ms + RMS_EPS) * g_ref[...]


def _ffn(h2, x1, wg, wu, wd, g_final, *, tm, tf):
    n, d = x1.shape
    dff = wg.shape[1]
    assert n % tm == 0 and dff % tf == 0
    return pl.pallas_call(
        _ffn_kernel,
        out_shape=jax.ShapeDtypeStruct((n, d), F32),
        grid=(n // tm, dff // tf),
        in_specs=[
            pl.BlockSpec((tm, d), lambda i, j: (i, 0)),
            pl.BlockSpec((tm, d), lambda i, j: (i, 0)),
            pl.BlockSpec((d, tf), lambda i, j: (0, j)),
            pl.BlockSpec((d, tf), lambda i, j: (0, j)),
            pl.BlockSpec((tf, d), lambda i, j: (j, 0)),
            pl.BlockSpec((1, d), lambda i, j: (0, 0)),
        ],
        out_specs=pl.BlockSpec((tm, d), lambda i, j: (i, 0)),
        compiler_params=_cparams(("parallel", "arbitrary")),
        name="swiglu_ffn",
    )(h2, x1, wg, wu, wd, g_final.reshape(1, d))


def _rope_tables(seq):
    half = ROT_DIM // 2
    pos = jnp.arange(seq, dtype=F32)
    inv_freq = ROPE_THETA ** (-jnp.arange(0, ROT_DIM, 2, dtype=F32) / ROT_DIM)
    ang = pos[:, None] * inv_freq[None, :]
    cos, sin = jnp.cos(ang), jnp.sin(ang)
    ones = jnp.ones((seq, HEAD_DIM - ROT_DIM), F32)
    zeros = lambda w: jnp.zeros((seq, w), F32)
    rope_c = jnp.concatenate([cos, cos, ones], axis=1)
    rope_sa = jnp.concatenate([-sin, zeros(HEAD_DIM - half)], axis=1)
    rope_sb = jnp.concatenate([zeros(half), sin, zeros(HEAD_DIM - ROT_DIM)], axis=1)
    q_scale = math.log2(math.e) / math.sqrt(HEAD_DIM)
    return tuple(jnp.stack([t * q_scale, t]) for t in (rope_c, rope_sa, rope_sb))


def _tile(n, pref):
    t = min(n, pref)
    while n % t:
        t //= 2
    return t


def kernel(x, g_mix, w_in, lambda_q1, lambda_k1, lambda_q2, lambda_k2, g_subln, w_attn_branch,
           w_four_branch, w_out, g_ffn, w_gate, w_up, w_down, g_final):
    b, s, d = x.shape
    n = b * s
    v_width = w_attn_branch.shape[0]
    f_width = w_four_branch.shape[0]
    vd = g_subln.shape[0]
    n_heads = v_width // vd
    qk_width = n_heads * 2 * HEAD_DIM
    f_start = 2 * qk_width + v_width
    gate_start = f_start + f_width
    assert vd == 2 * HEAD_DIM and w_in.shape[1] == gate_start + 2 * d
    assert v_width % f_width == 0 and s % (FFT_RADIX * 2 * V7X_SUBLANES) == 0

    x2 = x.reshape(n, d)
    tm = _tile(s, 1024)
    vf, h = _proj_norm(x2, g_mix, w_in, 2 * qk_width, v_width + f_width, tm=_tile(n, 512), tn=1024)
    qk, wa_bf, wf_bf, wo_bf = _proj_rope(h, w_in, 0, 2 * qk_width, _rope_tables(s),
                                         (w_attn_branch, w_four_branch, w_out), seq=s, q_width=qk_width,
                                         tm=tm, tn=qk_width)
    gates, wg_bf, wu_bf, wd_bf = _proj_gates(h, w_in, gate_start, 2 * d, w_gate, w_up, w_down, tm=tm, tn=1024)

    lamv = jnp.stack([lambda_q1, lambda_k1, lambda_q2, lambda_k2]).astype(F32)
    o = _attention(qk.reshape(b, s, -1), vf.reshape(b, s, -1), lamv, g_subln, n_heads=n_heads, tq=256,
                   n_steps=min(16, s // 256))

    ma, twc, tws, cd = _fourier_tables(s)
    fm = _fourier(vf.reshape(b, s // FFT_RADIX, -1), ma, twc, tws, cd, f_col_start=v_width, c=f_width)

    x1, h2 = _merge(o.reshape(n, v_width), fm.reshape(n, f_width), gates, x2, wa_bf, wf_bf, wo_bf, g_ffn,
                    tm=_tile(n, 512))

    out = _ffn(h2, x1, wg_bf, wu_bf, wd_bf, g_final, tm=_tile(n, 1024), tf=512)
    return out.reshape(b, s, d)
```

```python
import functools
import math

import jax
import jax.numpy as jnp
from jax import lax
from jax.experimental import pallas as pl
from jax.experimental.pallas import tpu as pltpu

F32 = jnp.float32
BF16 = jnp.bfloat16

HEAD_DIM = 128
ROT_DIM = HEAD_DIM // 4
ROPE_THETA = 500000.0
LAMBDA_INIT = 0.8 - 0.6 * math.exp(-0.3 * 0)
FOURIER_GROUP_DIM = 128
RMS_EPS = 1e-5
FFT_RADIX = 16

V7X_LANES = 128
V7X_SUBLANES = 8
V7X_VMEM_LIMIT_BYTES = 58 * 1024 * 1024


def _cparams(sem):
    return pltpu.CompilerParams(dimension_semantics=sem, vmem_limit_bytes=V7X_VMEM_LIMIT_BYTES)


def _proj_rope_kernel(h_ref, w_ref, cos_ref, sa_ref, sb_ref, *refs):
    n_side = len(refs) // 2
    z_ref = refs[n_side]
    for src, dst in zip(refs[:n_side], refs[n_side + 1:]):
        dst[...] = src[...].astype(BF16)

    acc = jnp.dot(h_ref[...], w_ref[...].astype(BF16), preferred_element_type=F32)
    tn = acc.shape[1]
    rep = tn // HEAD_DIM
    c = jnp.concatenate([cos_ref[0]] * rep, axis=1)
    sa = jnp.concatenate([sa_ref[0]] * rep, axis=1)
    sb = jnp.concatenate([sb_ref[0]] * rep, axis=1)
    half = ROT_DIM // 2
    r = acc * c + pltpu.roll(acc, tn - half, 1) * sa + pltpu.roll(acc, half, 1) * sb
    z_ref[...] = r.astype(BF16)


def _proj_norm_kernel(x_ref, g_ref, *refs, f_col):
    w_refs, (z_ref, h_ref, ft_ref, fs_ref) = refs[:-4], refs[-4:]
    x = x_ref[...]
    ms = jnp.mean(x * x, axis=-1, keepdims=True)
    h = (x * lax.rsqrt(ms + RMS_EPS) * g_ref[...]).astype(BF16)
    h_ref[...] = h
    tm, tn = x.shape[0], w_refs[0].shape[1]
    _, radix, rows, fw = ft_ref.shape
    for k, w_ref in enumerate(w_refs):
        res = jnp.dot(h, w_ref[...].astype(BF16), preferred_element_type=F32)
        z_ref[:, k * tn:(k + 1) * tn] = res.astype(BF16)
        if k == f_col // tn:
            for q in range(fw // V7X_LANES):
                lo = f_col % tn + q * V7X_LANES
                fs_ref[q] = res[:, lo:lo + V7X_LANES]
    for r in range(radix):
        ft_ref[0, r] = jnp.concatenate([fs_ref[q, pl.ds(r, rows, stride=radix), :]
                                        for q in range(fw // V7X_LANES)], axis=1).astype(BF16)


def _proj_norm(x2, g, w, col_start, width, *, seq, f_col, f_width, tm, tn):
    n, d = x2.shape
    assert n % tm == 0 and width % tn == 0 and col_start % tn == 0
    assert seq % tm == 0 and tm % (16 * FFT_RADIX) == 0 and f_col % tn + f_width <= tn
    col_off = col_start // tn
    s_tiles, rows = seq // tm, tm // FFT_RADIX
    w_specs = [pl.BlockSpec((d, tn), functools.partial(lambda i, k: (0, k), k=col_off + k),
                            pipeline_mode=pl.Buffered(1)) for k in range(width // tn)]
    return pl.pallas_call(
        functools.partial(_proj_norm_kernel, f_col=f_col),
        out_shape=(jax.ShapeDtypeStruct((n, width), BF16), jax.ShapeDtypeStruct((n, d), BF16),
                   jax.ShapeDtypeStruct((n // seq, FFT_RADIX, seq // FFT_RADIX, f_width), BF16)),
        grid=(n // tm,),
        in_specs=[pl.BlockSpec((tm, d), lambda i: (i, 0)), pl.BlockSpec((1, d), lambda i: (0, 0))] + w_specs,
        out_specs=(pl.BlockSpec((tm, width), lambda i: (i, 0)), pl.BlockSpec((tm, d), lambda i: (i, 0)),
                   pl.BlockSpec((1, FFT_RADIX, rows, f_width), lambda i: (i // s_tiles, 0, i % s_tiles, 0))),
        scratch_shapes=[pltpu.VMEM((f_width // V7X_LANES, tm, V7X_LANES), F32)],
        compiler_params=_cparams(("parallel",)),
        name="proj_vf_norm",
    )(x2, g.reshape(1, d), *([w] * (width // tn)))


def _proj_sigmoid_kernel(h_ref, w_ref, wg_ref, wu_ref, wd_ref, z_ref, wg_out, wu_out, wd_out):
    wg_out[...] = wg_ref[...].astype(BF16)
    wu_out[...] = wu_ref[...].astype(BF16)

    @pl.when(pl.program_id(0) == 0)
    def _():
        wd_out[...] = wd_ref[...].astype(BF16)

    acc = jnp.dot(h_ref[...], w_ref[...].astype(BF16), preferred_element_type=F32)
    z_ref[...] = jax.nn.sigmoid(acc).astype(BF16)


def _proj_gates(h, w, col_start, width, w_gate, w_up, w_down, *, tm, tn):
    n, d = h.shape
    dff = w_gate.shape[1]
    ni, nj = n // tm, width // tn
    assert n % tm == 0 and width % tn == 0 and col_start % tn == 0
    rg, rd = d // (ni * nj), dff // ni
    assert rg * ni * nj == d and rd * ni == dff and rg % 16 == 0 and rd % 16 == 0
    col_off = col_start // tn
    step = lambda j, i: (j * ni + i, 0)
    down = lambda j, i: (jnp.where(j == 0, i, ni - 1), 0)
    return pl.pallas_call(
        _proj_sigmoid_kernel,
        out_shape=(jax.ShapeDtypeStruct((n, width), BF16), jax.ShapeDtypeStruct(w_gate.shape, BF16),
                   jax.ShapeDtypeStruct(w_up.shape, BF16), jax.ShapeDtypeStruct(w_down.shape, BF16)),
        grid=(nj, ni),
        in_specs=[pl.BlockSpec((tm, d), lambda j, i: (i, 0)), pl.BlockSpec((d, tn), lambda j, i: (0, j + col_off)),
                  pl.BlockSpec((rg, dff), step), pl.BlockSpec((rg, dff), step), pl.BlockSpec((rd, d), down)],
        out_specs=(pl.BlockSpec((tm, tn), lambda j, i: (i, j)), pl.BlockSpec((rg, dff), step),
                   pl.BlockSpec((rg, dff), step), pl.BlockSpec((rd, d), down)),
        compiler_params=_cparams(("arbitrary", "arbitrary")),
        name="proj_gates",
    )(h, w, w_gate, w_up, w_down)


def _proj_rope(h, w, col_start, width, rope, side_weights, *, seq, q_width, tm, tn):
    n, d = h.shape
    assert n % tm == 0 and width % tn == 0 and col_start % tn == 0
    assert seq % tm == 0 and q_width % tn == 0 and tn % HEAD_DIM == 0
    col_off = col_start // tn
    s_tiles, q_tiles = seq // tm, q_width // tn
    table = pl.BlockSpec((1, tm, HEAD_DIM), lambda j, i: (j // q_tiles, i % s_tiles, 0))
    nj, ni = width // tn, n // tm
    side_specs = []
    for sw in side_weights:
        rows = sw.shape[0] // (nj * ni)
        assert rows * nj * ni == sw.shape[0] and rows % 16 == 0
        side_specs.append(pl.BlockSpec((rows, sw.shape[1]), lambda j, i: (j * ni + i, 0)))
    return pl.pallas_call(
        _proj_rope_kernel,
        out_shape=(jax.ShapeDtypeStruct((n, width), BF16),
                   *[jax.ShapeDtypeStruct(sw.shape, BF16) for sw in side_weights]),
        grid=(nj, ni),
        in_specs=[pl.BlockSpec((tm, d), lambda j, i: (i, 0)), pl.BlockSpec((d, tn), lambda j, i: (0, j + col_off)),
                  table, table, table] + side_specs,
        out_specs=(pl.BlockSpec((tm, tn), lambda j, i: (i, j)), *side_specs),
        compiler_params=_cparams(("arbitrary", "arbitrary")),
        name="proj_qk_rope",
    )(h, w, *rope, *side_weights)


_NT = (((1,), (1,)), ((), ()))


def _lane_fold(x, op, ways=4):
    n = x.shape[1] // V7X_LANES
    ways = min(ways, n)
    parts = [x[:, i * V7X_LANES:(i + 1) * V7X_LANES] for i in range(ways)]
    for i in range(ways, n):
        parts[i % ways] = op(parts[i % ways], x[:, i * V7X_LANES:(i + 1) * V7X_LANES])
    while len(parts) > 1:
        parts = [op(parts[2 * i], parts[2 * i + 1]) for i in range(len(parts) // 2)] + parts[len(parts) // 2 * 2:]
    return parts[0]


def _tile_rows(tile, tq):
    if isinstance(tile, int):
        return slice(tile * tq, (tile + 1) * tq)
    return pl.ds(pl.multiple_of(tile * tq, tq), tq)


def _attn_kernel(lamv_ref, g_ref, q_ref, k_ref, v_ref, o_ref, s_ref, p_ref, m_ref, mb_ref, l_ref, r_ref, il_ref,
                 *, tq, n_steps):
    seq = k_ref.shape[1]
    n_tiles = seq // tq
    tk = seq // n_steps
    tr = tq // n_steps
    lv = lamv_ref[...]
    lam = (jnp.exp(jnp.sum(lv[0:1] * lv[1:2], axis=-1, keepdims=True))
           - jnp.exp(jnp.sum(lv[2:3] * lv[3:4], axis=-1, keepdims=True)) + LAMBDA_INIT)
    gain = g_ref[...] * (1.0 - LAMBDA_INIT)

    def qk_piece(tile, slot, j):
        rows = _tile_rows(tile, tq)
        cols = slice(j * tk, (j + 1) * tk)
        for c in range(2):
            qc = q_ref[0, rows, c * HEAD_DIM:(c + 1) * HEAD_DIM]
            kc = k_ref[0, cols, c * HEAD_DIM:(c + 1) * HEAD_DIM]
            s = lax.dot_general(qc, kc, _NT, preferred_element_type=F32)
            s_ref[slot, c, :, cols] = s
            mx = _lane_fold(s, jnp.maximum)
            m_ref[slot, c] = mx if j == 0 else jnp.maximum(m_ref[slot, c], mx)

    def exp_prep(slot):
        for c in range(2):
            m = jnp.max(m_ref[slot, c], axis=-1, keepdims=True)
            mb_ref[c] = jnp.broadcast_to(m, (tq, V7X_LANES))

    def exp_piece(slot, i, pace):
        rs = slice(i * tr, (i + 1) * tr)
        for c in range(2):
            mrow = mb_ref[c, rs, :]
            if pace is not None:
                mrow = mrow + 0.0 * jnp.concatenate([pace] * (tr // V7X_SUBLANES), axis=0)
            m = jnp.concatenate([mrow] * (seq // V7X_LANES), axis=1)
            p = jnp.exp2(s_ref[slot, c, rs, :] - m)
            p_ref[slot, c, rs, :] = p.astype(BF16)
            l_ref[slot, c, rs, :] = _lane_fold(p, jnp.add)

    def pv_prep(slot):
        l1 = jnp.sum(l_ref[slot, 0], axis=-1, keepdims=True)
        l2 = jnp.sum(l_ref[slot, 1], axis=-1, keepdims=True)
        r_ref[...] = jnp.broadcast_to(lam * l1 / l2, (tq, V7X_LANES)).astype(BF16)
        il_ref[...] = jnp.broadcast_to(1.0 / l1, (tq, V7X_LANES))

    def pv_piece(slot, i, acc):
        cols = slice(i * tk, (i + 1) * tk)
        r = jnp.concatenate([r_ref[...]] * (tk // V7X_LANES), axis=1)
        a = p_ref[slot, 0, :, cols] - r * p_ref[slot, 1, :, cols]
        d = jnp.dot(a, v_ref[0, cols, :], preferred_element_type=F32)
        return d if acc is None else acc + d

    def pv_finish(tile, acc):
        o = acc * jnp.concatenate([il_ref[...]] * (acc.shape[1] // V7X_LANES), axis=1)
        ms = jnp.mean(o * o, axis=-1, keepdims=True)
        o_ref[0, _tile_rows(tile, tq), :] = (o * lax.rsqrt(ms + RMS_EPS) * gain).astype(BF16)

    def period(p, par, do_qk, do_exp, do_pv):
        if do_exp:
            exp_prep(1 - par)
        if do_pv:
            pv_prep(par)
        acc = None
        pace = None
        for i in range(n_steps):
            if do_qk:
                qk_piece(p, par, i)
            if do_exp:
                exp_piece(1 - par, i, pace)
            if do_pv:
                acc = pv_piece(par, i, acc)
                pace = acc[0:V7X_SUBLANES, 0:V7X_LANES]
        if do_pv:
            pv_finish(p - 2, acc)

    period(0, 0, True, False, False)
    period(1, 1, True, True, False)

    def body(g, carry):
        period(2 * g + 2, 0, True, True, True)
        period(2 * g + 3, 1, True, True, True)
        return carry

    lax.fori_loop(0, (n_tiles - 2) // 2, body, 0)
    period(n_tiles, 0, False, True, True)
    period(n_tiles + 1, 1, False, False, True)


def _attention(qk3, vf3, lamv, g_subln, *, n_heads, tq, n_steps):
    b, s, _ = qk3.shape
    vd = 2 * HEAD_DIM
    assert s % (2 * tq) == 0 and (s // n_steps) % V7X_LANES == 0 and (tq // n_steps) % 16 == 0
    return pl.pallas_call(
        functools.partial(_attn_kernel, tq=tq, n_steps=n_steps),
        out_shape=jax.ShapeDtypeStruct((b, s, n_heads * vd), BF16),
        grid=(b, n_heads),
        in_specs=[
            pl.BlockSpec((4, HEAD_DIM), lambda bi, h: (0, 0)),
            pl.BlockSpec((1, vd), lambda bi, h: (0, 0)),
            pl.BlockSpec((1, s, vd), lambda bi, h: (bi, 0, h)),
            pl.BlockSpec((1, s, vd), lambda bi, h: (bi, 0, n_heads + h)),
            pl.BlockSpec((1, s, vd), lambda bi, h: (bi, 0, h)),
        ],
        out_specs=pl.BlockSpec((1, s, vd), lambda bi, h: (bi, 0, h)),
        scratch_shapes=[
            pltpu.VMEM((2, 2, tq, s), F32),
            pltpu.VMEM((2, 2, tq, s), BF16),
            pltpu.VMEM((2, 2, tq, V7X_LANES), F32),
            pltpu.VMEM((2, tq, V7X_LANES), F32),
            pltpu.VMEM((2, 2, tq, V7X_LANES), F32),
            pltpu.VMEM((tq, V7X_LANES), BF16),
            pltpu.VMEM((tq, V7X_LANES), F32),
        ],
        compiler_params=_cparams(("parallel", "parallel")),
        name="diff_attn",
    )(lamv, g_subln.reshape(1, vd), qk3, qk3, vf3)


def _cmul_const(x, wr, wi):
    xr, xi = x
    wr = 0.0 if abs(wr) < 1e-12 else wr
    wi = 0.0 if abs(wi) < 1e-12 else wi
    if wi == 0.0:
        return x if wr == 1.0 else ((-xr, -xi) if wr == -1.0 else (xr * wr, xi * wr))
    if wr == 0.0:
        return (xi, -xr) if wi == -1.0 else ((-xi, xr) if wi == 1.0 else (-xi * wi, xr * wi))
    return (xr * wr - xi * wi, xr * wi + xi * wr)


def _fft_list(xs):
    n = len(xs)
    if n == 1:
        return xs
    ev = _fft_list(xs[0::2])
    od = _fft_list(xs[1::2])
    out = [None] * n
    for k in range(n // 2):
        ang = -2.0 * math.pi * k / n
        tr, ti = _cmul_const(od[k], math.cos(ang), math.sin(ang))
        out[k] = (ev[k][0] + tr, ev[k][1] + ti)
        out[k + n // 2] = (ev[k][0] - tr, ev[k][1] - ti)
    return out


def _fourier_kernel(f_ref, ma_ref, twc_ref, tws_ref, cd_ref, out_ref, y_ref):
    _, n1, n2, c = f_ref.shape
    for r in range(n1):
        y = jnp.dot(ma_ref[...], f_ref[0, r], preferred_element_type=F32)
        y_ref[r, :, :c] = y[:n2]
        y_ref[r, :, c:] = y[n2:]

    def body(t, carry):
        rows = pl.ds(pl.multiple_of(t * V7X_SUBLANES, V7X_SUBLANES), V7X_SUBLANES)
        for lc in range(c // V7X_LANES):
            re_l = slice(lc * V7X_LANES, (lc + 1) * V7X_LANES)
            im_l = slice(c + lc * V7X_LANES, c + (lc + 1) * V7X_LANES)
            zs = []
            for r in range(n1):
                yr, yi = y_ref[r, rows, re_l], y_ref[r, rows, im_l]
                tc, ts = twc_ref[r, rows, :], tws_ref[r, rows, :]
                zs.append((yr * tc + yi * ts, yi * tc - yr * ts))
            for k2, (xr, xi) in enumerate(_fft_list(zs)):
                y_ref[k2, rows, re_l] = xr
                y_ref[k2, rows, im_l] = xi
        return carry

    lax.fori_loop(0, n2 // V7X_SUBLANES, body, 0)

    gd = FOURIER_GROUP_DIM
    for k2 in range(n1):
        xk = y_ref[k2].astype(BF16)
        parts = []
        for grp in range(c // gd):
            lhs = jnp.concatenate([xk[:, grp * gd:(grp + 1) * gd], xk[:, c + grp * gd:c + (grp + 1) * gd]], axis=1)
            parts.append(jnp.dot(lhs, cd_ref[...], preferred_element_type=F32))
        out_ref[0, k2 * n2:(k2 + 1) * n2, :] = jnp.concatenate(parts, axis=1).astype(BF16)


def _fourier(ft, ma, twc, tws, cd):
    b, n1, n2, c = ft.shape
    return pl.pallas_call(
        _fourier_kernel,
        out_shape=jax.ShapeDtypeStruct((b, n1 * n2, c), BF16),
        grid=(b,),
        in_specs=[
            pl.BlockSpec((1, n1, n2, c), lambda bi: (bi, 0, 0, 0)),
            pl.BlockSpec(ma.shape, lambda bi: (0, 0)),
            pl.BlockSpec(twc.shape, lambda bi: (0, 0, 0)),
            pl.BlockSpec(tws.shape, lambda bi: (0, 0, 0)),
            pl.BlockSpec(cd.shape, lambda bi: (0, 0)),
        ],
        out_specs=pl.BlockSpec((1, n1 * n2, c), lambda bi: (bi, 0, 0)),
        scratch_shapes=[pltpu.VMEM((n1, n2, 2 * c), F32)],
        compiler_params=_cparams(("parallel",)),
        name="fourier_mix",
    )(ft, ma, twc, tws, cd)


def _fourier_tables(seq):
    n1, n2 = FFT_RADIX, seq // FFT_RADIX
    gd = FOURIER_GROUP_DIM

    def cos_sin(n, scale):
        idx = jnp.arange(n, dtype=jnp.int32)
        ang = ((idx[:, None] * idx[None, :]) % n).astype(F32) * (2.0 * math.pi / n)
        return jnp.cos(ang) * scale, jnp.sin(ang) * scale

    c2, s2 = cos_sin(n2, seq ** -0.5)
    ma = jnp.concatenate([c2, -s2], axis=0).astype(BF16)
    cg, sg = cos_sin(gd, gd ** -0.5)
    cd = jnp.concatenate([cg, sg], axis=0).astype(BF16)
    r = jnp.arange(n1, dtype=jnp.int32)[:, None]
    k1 = jnp.arange(n2, dtype=jnp.int32)[None, :]
    ang = (r * k1).astype(F32) * (2.0 * math.pi / seq)
    twc = jnp.broadcast_to(jnp.cos(ang)[:, :, None], (n1, n2, V7X_LANES))
    tws = jnp.broadcast_to(jnp.sin(ang)[:, :, None], (n1, n2, V7X_LANES))
    return ma, twc, tws, cd


def _merge_kernel(o_ref, fm_ref, ga0_ref, ga1_ref, gf0_ref, gf1_ref, x_ref, wa_ref, wf_ref, wo_ref,
                  g_ref, x1_ref, h2_ref):
    o = o_ref[...]
    fm = fm_ref[...]
    acc = x_ref[...]
    half = ga0_ref.shape[1]
    for ci, (ga_ref, gf_ref) in enumerate(((ga0_ref, gf0_ref), (ga1_ref, gf1_ref))):
        cols = slice(ci * half, (ci + 1) * half)
        br_a = jnp.dot(o, wa_ref[:, cols], preferred_element_type=F32)
        br_f = jnp.dot(fm, wf_ref[:, cols], preferred_element_type=F32)
        m = ga_ref[...].astype(F32) * br_a + gf_ref[...].astype(F32) * br_f
        acc = acc + jnp.dot(m.astype(BF16), wo_ref[cols, :], preferred_element_type=F32)
    x1_ref[...] = acc
    ms = jnp.mean(acc * acc, axis=-1, keepdims=True)
    h2_ref[...] = (acc * lax.rsqrt(ms + RMS_EPS) * g_ref[...]).astype(BF16)


def _merge(o2, fm2, gates, x2, wa, wf, wo, g_ffn, *, tm):
    n, d = x2.shape
    half = d // 2
    assert gates.shape[1] == 2 * d and n % tm == 0
    const = lambda i: (0, 0)
    return pl.pallas_call(
        _merge_kernel,
        out_shape=(jax.ShapeDtypeStruct((n, d), F32), jax.ShapeDtypeStruct((n, d), BF16)),
        grid=(n // tm,),
        in_specs=[
            pl.BlockSpec((tm, o2.shape[1]), lambda i: (i, 0)),
            pl.BlockSpec((tm, fm2.shape[1]), lambda i: (i, 0)),
            pl.BlockSpec((tm, half), lambda i: (i, 0)),
            pl.BlockSpec((tm, half), lambda i: (i, 1)),
            pl.BlockSpec((tm, half), lambda i: (i, 2)),
            pl.BlockSpec((tm, half), lambda i: (i, 3)),
            pl.BlockSpec((tm, d), lambda i: (i, 0)),
            pl.BlockSpec(wa.shape, const, pipeline_mode=pl.Buffered(1)),
            pl.BlockSpec(wf.shape, const, pipeline_mode=pl.Buffered(1)),
            pl.BlockSpec(wo.shape, const, pipeline_mode=pl.Buffered(1)),
            pl.BlockSpec((1, d), const),
        ],
        out_specs=(pl.BlockSpec((tm, d), lambda i: (i, 0)), pl.BlockSpec((tm, d), lambda i: (i, 0))),
        compiler_params=_cparams(("parallel",)),
        name="merge_out_proj",
    )(o2, fm2, gates, gates, gates, gates, x2, wa, wf, wo, g_ffn.reshape(1, d))


def _ffn_kernel(h_ref, x1_ref, wg_ref, wu_ref, wd_ref, g_ref, out_ref):
    j = pl.program_id(1)

    @pl.when(j == 0)
    def _():
        out_ref[...] = x1_ref[...]

    h = h_ref[...]
    gate = jnp.dot(h, wg_ref[...], preferred_element_type=F32)
    up = jnp.dot(h, wu_ref[...], preferred_element_type=F32)
    act = (gate * jax.nn.sigmoid(gate) * up).astype(BF16)
    out_ref[...] += jnp.dot(act, wd_ref[...], preferred_element_type=F32)

    @pl.when(j == pl.num_programs(1) - 1)
    def _():
        y = out_ref[...]
        ms = jnp.mean(y * y, axis=-1, keepdims=True)
        out_ref[...] = y * lax.rsqrt(ms + RMS_EPS) * g_ref[...]


def _ffn(h2, x1, wg, wu, wd, g_final, *, tm, tf):
    n, d = x1.shape
    dff = wg.shape[1]
    assert n % tm == 0 and dff % tf == 0
    return pl.pallas_call(
        _ffn_kernel,
        out_shape=jax.ShapeDtypeStruct((n, d), F32),
        grid=(n // tm, dff // tf),
        in_specs=[
            pl.BlockSpec((tm, d), lambda i, j: (i, 0)),
            pl.BlockSpec((tm, d), lambda i, j: (i, 0)),
            pl.BlockSpec((d, tf), lambda i, j: (0, j)),
            pl.BlockSpec((d, tf), lambda i, j: (0, j)),
            pl.BlockSpec((tf, d), lambda i, j: (j, 0)),
            pl.BlockSpec((1, d), lambda i, j: (0, 0)),
        ],
        out_specs=pl.BlockSpec((tm, d), lambda i, j: (i, 0)),
        compiler_params=_cparams(("parallel", "arbitrary")),
        name="swiglu_ffn",
    )(h2, x1, wg, wu, wd, g_final.reshape(1, d))


def _rope_tables(seq):
    half = ROT_DIM // 2
    pos = jnp.arange(seq, dtype=F32)
    inv_freq = ROPE_THETA ** (-jnp.arange(0, ROT_DIM, 2, dtype=F32) / ROT_DIM)
    ang = pos[:, None] * inv_freq[None, :]
    cos, sin = jnp.cos(ang), jnp.sin(ang)
    ones = jnp.ones((seq, HEAD_DIM - ROT_DIM), F32)
    zeros = lambda w: jnp.zeros((seq, w), F32)
    rope_c = jnp.concatenate([cos, cos, ones], axis=1)
    rope_sa = jnp.concatenate([-sin, zeros(HEAD_DIM - half)], axis=1)
    rope_sb = jnp.concatenate([zeros(half), sin, zeros(HEAD_DIM - ROT_DIM)], axis=1)
    q_scale = math.log2(math.e) / math.sqrt(HEAD_DIM)
    return tuple(jnp.stack([t * q_scale, t]) for t in (rope_c, rope_sa, rope_sb))


def _tile(n, pref):
    t = min(n, pref)
    while n % t:
        t //= 2
    return t


def kernel(x, g_mix, w_in, lambda_q1, lambda_k1, lambda_q2, lambda_k2, g_subln, w_attn_branch,
           w_four_branch, w_out, g_ffn, w_gate, w_up, w_down, g_final):
    b, s, d = x.shape
    n = b * s
    v_width = w_attn_branch.shape[0]
    f_width = w_four_branch.shape[0]
    vd = g_subln.shape[0]
    n_heads = v_width // vd
    qk_width = n_heads * 2 * HEAD_DIM
    f_start = 2 * qk_width + v_width
    gate_start = f_start + f_width
    assert vd == 2 * HEAD_DIM and w_in.shape[1] == gate_start + 2 * d
    assert v_width % f_width == 0 and s % (FFT_RADIX * 2 * V7X_SUBLANES) == 0

    x2 = x.reshape(n, d)
    tm = _tile(s, 1024)
    vf, h, ft = _proj_norm(x2, g_mix, w_in, 2 * qk_width, v_width + f_width, seq=s, f_col=v_width,
                           f_width=f_width, tm=_tile(s, 512), tn=1024)
    qk, wa_bf, wf_bf, wo_bf = _proj_rope(h, w_in, 0, 2 * qk_width, _rope_tables(s),
                                         (w_attn_branch, w_four_branch, w_out), seq=s, q_width=qk_width,
                                         tm=tm, tn=qk_width)
    gates, wg_bf, wu_bf, wd_bf = _proj_gates(h, w_in, gate_start, 2 * d, w_gate, w_up, w_down, tm=tm, tn=1024)

    lamv = jnp.stack([lambda_q1, lambda_k1, lambda_q2, lambda_k2]).astype(F32)
    o = _attention(qk.reshape(b, s, -1), vf.reshape(b, s, -1), lamv, g_subln, n_heads=n_heads, tq=256,
                   n_steps=min(16, s // 256))

    ma, twc, tws, cd = _fourier_tables(s)
    fm = _fourier(ft, ma, twc, tws, cd)

    x1, h2 = _merge(o.reshape(n, v_width), fm.reshape(n, f_width), gates, x2, wa_bf, wf_bf, wo_bf, g_ffn,
                    tm=_tile(n, 512))

    out = _ffn(h2, x1, wg_bf, wu_bf, wd_bf, g_final, tm=_tile(n, 1024), tf=512)
    return out.reshape(b, s, d)
```

```python
import functools
import math

import jax
import jax.numpy as jnp
from jax import lax
from jax.experimental import pallas as pl
from jax.experimental.pallas import tpu as pltpu

F32 = jnp.float32
BF16 = jnp.bfloat16

HEAD_DIM = 128
ROT_DIM = HEAD_DIM // 4
ROPE_THETA = 500000.0
LAMBDA_INIT = 0.8 - 0.6 * math.exp(-0.3 * 0)
FOURIER_GROUP_DIM = 128
RMS_EPS = 1e-5
FFT_RADIX = 16

V7X_LANES = 128
V7X_SUBLANES = 8
V7X_VMEM_LIMIT_BYTES = 58 * 1024 * 1024


def _cparams(sem):
    return pltpu.CompilerParams(dimension_semantics=sem, vmem_limit_bytes=V7X_VMEM_LIMIT_BYTES)


def _proj_rope_kernel(h_ref, w_ref, cos_ref, sa_ref, sb_ref, *refs):
    n_side = len(refs) // 2
    z_ref = refs[n_side]
    for src, dst in zip(refs[:n_side], refs[n_side + 1:]):
        dst[...] = src[...].astype(BF16)

    acc = jnp.dot(h_ref[...], w_ref[...].astype(BF16), preferred_element_type=F32)
    tn = acc.shape[1]
    rep = tn // HEAD_DIM
    c = jnp.concatenate([cos_ref[0]] * rep, axis=1)
    sa = jnp.concatenate([sa_ref[0]] * rep, axis=1)
    sb = jnp.concatenate([sb_ref[0]] * rep, axis=1)
    half = ROT_DIM // 2
    r = acc * c + pltpu.roll(acc, tn - half, 1) * sa + pltpu.roll(acc, half, 1) * sb
    z_ref[...] = r.astype(BF16)


def _proj_norm_kernel(x_ref, g_ref, *refs, f_col):
    w_refs, (z_ref, h_ref, ft_ref, fs_ref) = refs[:-4], refs[-4:]
    x = x_ref[...]
    ms = jnp.mean(x * x, axis=-1, keepdims=True)
    h = (x * lax.rsqrt(ms + RMS_EPS) * g_ref[...]).astype(BF16)
    h_ref[...] = h
    tm, tn = x.shape[0], w_refs[0].shape[1]
    _, radix, rows, fw = ft_ref.shape
    for k, w_ref in enumerate(w_refs):
        res = jnp.dot(h, w_ref[...].astype(BF16), preferred_element_type=F32)
        z_ref[:, k * tn:(k + 1) * tn] = res.astype(BF16)
        if k == f_col // tn:
            for q in range(fw // V7X_LANES):
                lo = f_col % tn + q * V7X_LANES
                fs_ref[q] = res[:, lo:lo + V7X_LANES]
    for r in range(radix):
        ft_ref[0, r] = jnp.concatenate([fs_ref[q, pl.ds(r, rows, stride=radix), :]
                                        for q in range(fw // V7X_LANES)], axis=1).astype(BF16)


def _proj_norm(x2, g, w, col_start, width, *, seq, f_col, f_width, tm, tn):
    n, d = x2.shape
    assert n % tm == 0 and width % tn == 0 and col_start % tn == 0
    assert seq % tm == 0 and tm % (16 * FFT_RADIX) == 0 and f_col % tn + f_width <= tn
    col_off = col_start // tn
    s_tiles, rows = seq // tm, tm // FFT_RADIX
    w_specs = [pl.BlockSpec((d, tn), functools.partial(lambda i, k: (0, k), k=col_off + k),
                            pipeline_mode=pl.Buffered(1)) for k in range(width // tn)]
    return pl.pallas_call(
        functools.partial(_proj_norm_kernel, f_col=f_col),
        out_shape=(jax.ShapeDtypeStruct((n, width), BF16), jax.ShapeDtypeStruct((n, d), BF16),
                   jax.ShapeDtypeStruct((n // seq, FFT_RADIX, seq // FFT_RADIX, f_width), BF16)),
        grid=(n // tm,),
        in_specs=[pl.BlockSpec((tm, d), lambda i: (i, 0)), pl.BlockSpec((1, d), lambda i: (0, 0))] + w_specs,
        out_specs=(pl.BlockSpec((tm, width), lambda i: (i, 0)), pl.BlockSpec((tm, d), lambda i: (i, 0)),
                   pl.BlockSpec((1, FFT_RADIX, rows, f_width), lambda i: (i // s_tiles, 0, i % s_tiles, 0))),
        scratch_shapes=[pltpu.VMEM((f_width // V7X_LANES, tm, V7X_LANES), F32)],
        compiler_params=_cparams(("parallel",)),
        name="proj_vf_norm",
    )(x2, g.reshape(1, d), *([w] * (width // tn)))


def _proj_sigmoid_kernel(h_ref, w_ref, wg_ref, wu_ref, wd_ref, z_ref, wg_out, wu_out, wd_out):
    wg_out[...] = wg_ref[...].astype(BF16)
    wu_out[...] = wu_ref[...].astype(BF16)

    @pl.when(pl.program_id(0) == 0)
    def _():
        wd_out[...] = wd_ref[...].astype(BF16)

    acc = jnp.dot(h_ref[...], w_ref[...].astype(BF16), preferred_element_type=F32)
    z_ref[...] = jax.nn.sigmoid(acc).astype(BF16)


def _proj_gates(h, w, col_start, width, w_gate, w_up, w_down, *, tm, tn):
    n, d = h.shape
    dff = w_gate.shape[1]
    ni, nj = n // tm, width // tn
    assert n % tm == 0 and width % tn == 0 and col_start % tn == 0
    rg, rd = d // (ni * nj), dff // ni
    assert rg * ni * nj == d and rd * ni == dff and rg % 16 == 0 and rd % 16 == 0
    col_off = col_start // tn
    step = lambda j, i: (j * ni + i, 0)
    down = lambda j, i: (jnp.where(j == 0, i, ni - 1), 0)
    return pl.pallas_call(
        _proj_sigmoid_kernel,
        out_shape=(jax.ShapeDtypeStruct((n, width), BF16), jax.ShapeDtypeStruct(w_gate.shape, BF16),
                   jax.ShapeDtypeStruct(w_up.shape, BF16), jax.ShapeDtypeStruct(w_down.shape, BF16)),
        grid=(nj, ni),
        in_specs=[pl.BlockSpec((tm, d), lambda j, i: (i, 0)), pl.BlockSpec((d, tn), lambda j, i: (0, j + col_off)),
                  pl.BlockSpec((rg, dff), step), pl.BlockSpec((rg, dff), step), pl.BlockSpec((rd, d), down)],
        out_specs=(pl.BlockSpec((tm, tn), lambda j, i: (i, j)), pl.BlockSpec((rg, dff), step),
                   pl.BlockSpec((rg, dff), step), pl.BlockSpec((rd, d), down)),
        compiler_params=_cparams(("arbitrary", "arbitrary")),
        name="proj_gates",
    )(h, w, w_gate, w_up, w_down)


def _proj_rope(h, w, col_start, width, rope, side_weights, *, seq, q_width, tm, tn):
    n, d = h.shape
    assert n % tm == 0 and width % tn == 0 and col_start % tn == 0
    assert seq % tm == 0 and q_width % tn == 0 and tn % HEAD_DIM == 0
    col_off = col_start // tn
    s_tiles, q_tiles = seq // tm, q_width // tn
    table = pl.BlockSpec((1, tm, HEAD_DIM), lambda j, i: (j // q_tiles, i % s_tiles, 0))
    nj, ni = width // tn, n // tm
    side_specs = []
    for sw in side_weights:
        rows = sw.shape[0] // (nj * ni)
        assert rows * nj * ni == sw.shape[0] and rows % 16 == 0
        side_specs.append(pl.BlockSpec((rows, sw.shape[1]), lambda j, i: (j * ni + i, 0)))
    return pl.pallas_call(
        _proj_rope_kernel,
        out_shape=(jax.ShapeDtypeStruct((n, width), BF16),
                   *[jax.ShapeDtypeStruct(sw.shape, BF16) for sw in side_weights]),
        grid=(nj, ni),
        in_specs=[pl.BlockSpec((tm, d), lambda j, i: (i, 0)), pl.BlockSpec((d, tn), lambda j, i: (0, j + col_off)),
                  table, table, table] + side_specs,
        out_specs=(pl.BlockSpec((tm, tn), lambda j, i: (i, j)), *side_specs),
        compiler_params=_cparams(("arbitrary", "arbitrary")),
        name="proj_qk_rope",
    )(h, w, *rope, *side_weights)


_NT = (((1,), (1,)), ((), ()))


def _lane_fold(x, op, ways=4):
    n = x.shape[1] // V7X_LANES
    ways = min(ways, n)
    parts = [x[:, i * V7X_LANES:(i + 1) * V7X_LANES] for i in range(ways)]
    for i in range(ways, n):
        parts[i % ways] = op(parts[i % ways], x[:, i * V7X_LANES:(i + 1) * V7X_LANES])
    while len(parts) > 1:
        parts = [op(parts[2 * i], parts[2 * i + 1]) for i in range(len(parts) // 2)] + parts[len(parts) // 2 * 2:]
    return parts[0]


def _tile_rows(tile, tq):
    if isinstance(tile, int):
        return slice(tile * tq, (tile + 1) * tq)
    return pl.ds(pl.multiple_of(tile * tq, tq), tq)


def _attn_kernel(lamv_ref, g_ref, q_ref, k_ref, v_ref, o_ref, s_ref, p_ref, m_ref, r_ref, il_ref,
                 *, tq, n_steps):
    seq = k_ref.shape[1]
    n_tiles = seq // tq
    tk = seq // n_steps
    tr = tq // n_steps
    lv = lamv_ref[...]
    lam = (jnp.exp(jnp.sum(lv[0:1] * lv[1:2], axis=-1, keepdims=True))
           - jnp.exp(jnp.sum(lv[2:3] * lv[3:4], axis=-1, keepdims=True)) + LAMBDA_INIT)
    gain = g_ref[...] * (1.0 - LAMBDA_INIT)

    def qk_piece(tile, slot, j):
        rows = _tile_rows(tile, tq)
        cols = slice(j * tk, (j + 1) * tk)
        for c in range(2):
            qc = q_ref[0, rows, c * HEAD_DIM:(c + 1) * HEAD_DIM]
            kc = k_ref[0, cols, c * HEAD_DIM:(c + 1) * HEAD_DIM]
            s = lax.dot_general(qc, kc, _NT, preferred_element_type=F32)
            s_ref[slot, c, :, cols] = s
            mx = _lane_fold(s, jnp.maximum)
            m_ref[slot, c] = mx if j == 0 else jnp.maximum(m_ref[slot, c], mx)

    def exp_piece(slot, i, pace):
        rs = slice(i * tr, (i + 1) * tr)
        tot = []
        for c in range(2):
            mrow = jnp.broadcast_to(jnp.max(m_ref[slot, c, rs, :], axis=-1, keepdims=True), (tr, V7X_LANES))
            if pace is not None:
                mrow = mrow + 0.0 * jnp.concatenate([pace] * (tr // V7X_SUBLANES), axis=0)
            m = jnp.concatenate([mrow] * (seq // V7X_LANES), axis=1)
            p = jnp.exp2(s_ref[slot, c, rs, :] - m)
            p_ref[slot, c, rs, :] = p.astype(BF16)
            tot.append(jnp.sum(_lane_fold(p, jnp.add), axis=-1, keepdims=True))
        r_ref[slot, rs, :] = jnp.broadcast_to(lam * tot[0] / tot[1], (tr, V7X_LANES)).astype(BF16)
        il_ref[slot, rs, :] = jnp.broadcast_to(1.0 / tot[0], (tr, V7X_LANES))

    def pv_piece(slot, i, acc):
        cols = slice(i * tk, (i + 1) * tk)
        r = jnp.concatenate([r_ref[slot]] * (tk // V7X_LANES), axis=1)
        a = p_ref[slot, 0, :, cols] - r * p_ref[slot, 1, :, cols]
        d = jnp.dot(a, v_ref[0, cols, :], preferred_element_type=F32)
        return d if acc is None else acc + d

    def pv_finish(tile, slot, acc):
        o = acc * jnp.concatenate([il_ref[slot]] * (acc.shape[1] // V7X_LANES), axis=1)
        ms = jnp.mean(o * o, axis=-1, keepdims=True)
        o_ref[0, _tile_rows(tile, tq), :] = (o * lax.rsqrt(ms + RMS_EPS) * gain).astype(BF16)

    def period(p, par, do_qk, do_exp, do_pv):
        acc = None
        pace = None
        for i in range(n_steps):
            if do_qk:
                qk_piece(p, par, i)
            if do_exp:
                exp_piece(1 - par, i, pace)
            if do_pv:
                acc = pv_piece(par, i, acc)
                pace = acc[0:V7X_SUBLANES, 0:V7X_LANES]
        if do_pv:
            pv_finish(p - 2, par, acc)

    period(0, 0, True, False, False)
    period(1, 1, True, True, False)

    def body(g, carry):
        period(2 * g + 2, 0, True, True, True)
        period(2 * g + 3, 1, True, True, True)
        return carry

    lax.fori_loop(0, (n_tiles - 2) // 2, body, 0)
    period(n_tiles, 0, False, True, True)
    period(n_tiles + 1, 1, False, False, True)


def _attention(qk3, vf3, lamv, g_subln, *, n_heads, tq, n_steps):
    b, s, _ = qk3.shape
    vd = 2 * HEAD_DIM
    assert s % (2 * tq) == 0 and (s // n_steps) % V7X_LANES == 0 and (tq // n_steps) % 16 == 0
    return pl.pallas_call(
        functools.partial(_attn_kernel, tq=tq, n_steps=n_steps),
        out_shape=jax.ShapeDtypeStruct((b, s, n_heads * vd), BF16),
        grid=(b, n_heads),
        in_specs=[
            pl.BlockSpec((4, HEAD_DIM), lambda bi, h: (0, 0)),
            pl.BlockSpec((1, vd), lambda bi, h: (0, 0)),
            pl.BlockSpec((1, s, vd), lambda bi, h: (bi, 0, h)),
            pl.BlockSpec((1, s, vd), lambda bi, h: (bi, 0, n_heads + h)),
            pl.BlockSpec((1, s, vd), lambda bi, h: (bi, 0, h)),
        ],
        out_specs=pl.BlockSpec((1, s, vd), lambda bi, h: (bi, 0, h)),
        scratch_shapes=[
            pltpu.VMEM((2, 2, tq, s), F32),
            pltpu.VMEM((2, 2, tq, s), BF16),
            pltpu.VMEM((2, 2, tq, V7X_LANES), F32),
            pltpu.VMEM((2, tq, V7X_LANES), BF16),
            pltpu.VMEM((2, tq, V7X_LANES), F32),
        ],
        compiler_params=_cparams(("parallel", "parallel")),
        name="diff_attn",
    )(lamv, g_subln.reshape(1, vd), qk3, qk3, vf3)


def _cmul_const(x, wr, wi):
    xr, xi = x
    wr = 0.0 if abs(wr) < 1e-12 else wr
    wi = 0.0 if abs(wi) < 1e-12 else wi
    if wi == 0.0:
        return x if wr == 1.0 else ((-xr, -xi) if wr == -1.0 else (xr * wr, xi * wr))
    if wr == 0.0:
        return (xi, -xr) if wi == -1.0 else ((-xi, xr) if wi == 1.0 else (-xi * wi, xr * wi))
    return (xr * wr - xi * wi, xr * wi + xi * wr)


def _fft_list(xs):
    n = len(xs)
    if n == 1:
        return xs
    ev = _fft_list(xs[0::2])
    od = _fft_list(xs[1::2])
    out = [None] * n
    for k in range(n // 2):
        ang = -2.0 * math.pi * k / n
        tr, ti = _cmul_const(od[k], math.cos(ang), math.sin(ang))
        out[k] = (ev[k][0] + tr, ev[k][1] + ti)
        out[k + n // 2] = (ev[k][0] - tr, ev[k][1] - ti)
    return out


def _fourier_kernel(f_ref, ma_ref, twc_ref, tws_ref, cd_ref, out_ref, y_ref):
    _, n1, n2, c = f_ref.shape
    for r in range(n1):
        y = jnp.dot(ma_ref[...], f_ref[0, r], preferred_element_type=F32)
        y_ref[r, :, :c] = y[:n2]
        y_ref[r, :, c:] = y[n2:]

    def body(t, carry):
        rows = pl.ds(pl.multiple_of(t * V7X_SUBLANES, V7X_SUBLANES), V7X_SUBLANES)
        for lc in range(c // V7X_LANES):
            re_l = slice(lc * V7X_LANES, (lc + 1) * V7X_LANES)
            im_l = slice(c + lc * V7X_LANES, c + (lc + 1) * V7X_LANES)
            zs = []
            for r in range(n1):
                yr, yi = y_ref[r, rows, re_l], y_ref[r, rows, im_l]
                tc, ts = twc_ref[r, rows, :], tws_ref[r, rows, :]
                zs.append((yr * tc + yi * ts, yi * tc - yr * ts))
            for k2, (xr, xi) in enumerate(_fft_list(zs)):
                y_ref[k2, rows, re_l] = xr
                y_ref[k2, rows, im_l] = xi
        return carry

    lax.fori_loop(0, n2 // V7X_SUBLANES, body, 0)

    gd = FOURIER_GROUP_DIM
    for k2 in range(n1):
        xk = y_ref[k2].astype(BF16)
        parts = []
        for grp in range(c // gd):
            lhs = jnp.concatenate([xk[:, grp * gd:(grp + 1) * gd], xk[:, c + grp * gd:c + (grp + 1) * gd]], axis=1)
            parts.append(jnp.dot(lhs, cd_ref[...], preferred_element_type=F32))
        out_ref[0, k2 * n2:(k2 + 1) * n2, :] = jnp.concatenate(parts, axis=1).astype(BF16)


def _fourier(ft, ma, twc, tws, cd):
    b, n1, n2, c = ft.shape
    return pl.pallas_call(
        _fourier_kernel,
        out_shape=jax.ShapeDtypeStruct((b, n1 * n2, c), BF16),
        grid=(b,),
        in_specs=[
            pl.BlockSpec((1, n1, n2, c), lambda bi: (bi, 0, 0, 0)),
            pl.BlockSpec(ma.shape, lambda bi: (0, 0)),
            pl.BlockSpec(twc.shape, lambda bi: (0, 0, 0)),
            pl.BlockSpec(tws.shape, lambda bi: (0, 0, 0)),
            pl.BlockSpec(cd.shape, lambda bi: (0, 0)),
        ],
        out_specs=pl.BlockSpec((1, n1 * n2, c), lambda bi: (bi, 0, 0)),
        scratch_shapes=[pltpu.VMEM((n1, n2, 2 * c), F32)],
        compiler_params=_cparams(("parallel",)),
        name="fourier_mix",
    )(ft, ma, twc, tws, cd)


def _fourier_tables(seq):
    n1, n2 = FFT_RADIX, seq // FFT_RADIX
    gd = FOURIER_GROUP_DIM

    def cos_sin(n, scale):
        idx = jnp.arange(n, dtype=jnp.int32)
        ang = ((idx[:, None] * idx[None, :]) % n).astype(F32) * (2.0 * math.pi / n)
        return jnp.cos(ang) * scale, jnp.sin(ang) * scale

    c2, s2 = cos_sin(n2, seq ** -0.5)
    ma = jnp.concatenate([c2, -s2], axis=0).astype(BF16)
    cg, sg = cos_sin(gd, gd ** -0.5)
    cd = jnp.concatenate([cg, sg], axis=0).astype(BF16)
    r = jnp.arange(n1, dtype=jnp.int32)[:, None]
    k1 = jnp.arange(n2, dtype=jnp.int32)[None, :]
    ang = (r * k1).astype(F32) * (2.0 * math.pi / seq)
    twc = jnp.broadcast_to(jnp.cos(ang)[:, :, None], (n1, n2, V7X_LANES))
    tws = jnp.broadcast_to(jnp.sin(ang)[:, :, None], (n1, n2, V7X_LANES))
    return ma, twc, tws, cd


def _merge_kernel(o_ref, fm_ref, ga0_ref, ga1_ref, gf0_ref, gf1_ref, x_ref, wa_ref, wf_ref, wo_ref,
                  g_ref, x1_ref, h2_ref):
    o = o_ref[...]
    fm = fm_ref[...]
    acc = x_ref[...]
    half = ga0_ref.shape[1]
    for ci, (ga_ref, gf_ref) in enumerate(((ga0_ref, gf0_ref), (ga1_ref, gf1_ref))):
        cols = slice(ci * half, (ci + 1) * half)
        br_a = jnp.dot(o, wa_ref[:, cols], preferred_element_type=F32)
        br_f = jnp.dot(fm, wf_ref[:, cols], preferred_element_type=F32)
        m = ga_ref[...].astype(F32) * br_a + gf_ref[...].astype(F32) * br_f
        acc = acc + jnp.dot(m.astype(BF16), wo_ref[cols, :], preferred_element_type=F32)
    x1_ref[...] = acc
    ms = jnp.mean(acc * acc, axis=-1, keepdims=True)
    h2_ref[...] = (acc * lax.rsqrt(ms + RMS_EPS) * g_ref[...]).astype(BF16)


def _merge(o2, fm2, gates, x2, wa, wf, wo, g_ffn, *, tm):
    n, d = x2.shape
    half = d // 2
    assert gates.shape[1] == 2 * d and n % tm == 0
    const = lambda i: (0, 0)
    return pl.pallas_call(
        _merge_kernel,
        out_shape=(jax.ShapeDtypeStruct((n, d), F32), jax.ShapeDtypeStruct((n, d), BF16)),
        grid=(n // tm,),
        in_specs=[
            pl.BlockSpec((tm, o2.shape[1]), lambda i: (i, 0)),
            pl.BlockSpec((tm, fm2.shape[1]), lambda i: (i, 0)),
            pl.BlockSpec((tm, half), lambda i: (i, 0)),
            pl.BlockSpec((tm, half), lambda i: (i, 1)),
            pl.BlockSpec((tm, half), lambda i: (i, 2)),
            pl.BlockSpec((tm, half), lambda i: (i, 3)),
            pl.BlockSpec((tm, d), lambda i: (i, 0)),
            pl.BlockSpec(wa.shape, const, pipeline_mode=pl.Buffered(1)),
            pl.BlockSpec(wf.shape, const, pipeline_mode=pl.Buffered(1)),
            pl.BlockSpec(wo.shape, const, pipeline_mode=pl.Buffered(1)),
            pl.BlockSpec((1, d), const),
        ],
        out_specs=(pl.BlockSpec((tm, d), lambda i: (i, 0)), pl.BlockSpec((tm, d), lambda i: (i, 0))),
        compiler_params=_cparams(("parallel",)),
        name="merge_out_proj",
    )(o2, fm2, gates, gates, gates, gates, x2, wa, wf, wo, g_ffn.reshape(1, d))


def _ffn_kernel(h_ref, x1_ref, wg_ref, wu_ref, wd_ref, g_ref, out_ref):
    j = pl.program_id(1)

    @pl.when(j == 0)
    def _():
        out_ref[...] = x1_ref[...]

    h = h_ref[...]
    gate = jnp.dot(h, wg_ref[...], preferred_element_type=F32)
    up = jnp.dot(h, wu_ref[...], preferred_element_type=F32)
    act = (gate * jax.nn.sigmoid(gate) * up).astype(BF16)
    out_ref[...] += jnp.dot(act, wd_ref[...], preferred_element_type=F32)

    @pl.when(j == pl.num_programs(1) - 1)
    def _():
        y = out_ref[...]
        ms = jnp.mean(y * y, axis=-1, keepdims=True)
        out_ref[...] = y * lax.rsqrt(ms + RMS_EPS) * g_ref[...]


def _ffn(h2, x1, wg, wu, wd, g_final, *, tm, tf):
    n, d = x1.shape
    dff = wg.shape[1]
    assert n % tm == 0 and dff % tf == 0
    return pl.pallas_call(
        _ffn_kernel,
        out_shape=jax.ShapeDtypeStruct((n, d), F32),
        grid=(n // tm, dff // tf),
        in_specs=[
            pl.BlockSpec((tm, d), lambda i, j: (i, 0)),
            pl.BlockSpec((tm, d), lambda i, j: (i, 0)),
            pl.BlockSpec((d, tf), lambda i, j: (0, j)),
            pl.BlockSpec((d, tf), lambda i, j: (0, j)),
            pl.BlockSpec((tf, d), lambda i, j: (j, 0)),
            pl.BlockSpec((1, d), lambda i, j: (0, 0)),
        ],
        out_specs=pl.BlockSpec((tm, d), lambda i, j: (i, 0)),
        compiler_params=_cparams(("parallel", "arbitrary")),
        name="swiglu_ffn",
    )(h2, x1, wg, wu, wd, g_final.reshape(1, d))


def _rope_tables(seq):
    half = ROT_DIM // 2
    pos = jnp.arange(seq, dtype=F32)
    inv_freq = ROPE_THETA ** (-jnp.arange(0, ROT_DIM, 2, dtype=F32) / ROT_DIM)
    ang = pos[:, None] * inv_freq[None, :]
    cos, sin = jnp.cos(ang), jnp.sin(ang)
    ones = jnp.ones((seq, HEAD_DIM - ROT_DIM), F32)
    zeros = lambda w: jnp.zeros((seq, w), F32)
    rope_c = jnp.concatenate([cos, cos, ones], axis=1)
    rope_sa = jnp.concatenate([-sin, zeros(HEAD_DIM - half)], axis=1)
    rope_sb = jnp.concatenate([zeros(half), sin, zeros(HEAD_DIM - ROT_DIM)], axis=1)
    q_scale = math.log2(math.e) / math.sqrt(HEAD_DIM)
    return tuple(jnp.stack([t * q_scale, t]) for t in (rope_c, rope_sa, rope_sb))


def _tile(n, pref):
    t = min(n, pref)
    while n % t:
        t //= 2
    return t


def kernel(x, g_mix, w_in, lambda_q1, lambda_k1, lambda_q2, lambda_k2, g_subln, w_attn_branch,
           w_four_branch, w_out, g_ffn, w_gate, w_up, w_down, g_final):
    b, s, d = x.shape
    n = b * s
    v_width = w_attn_branch.shape[0]
    f_width = w_four_branch.shape[0]
    vd = g_subln.shape[0]
    n_heads = v_width // vd
    qk_width = n_heads * 2 * HEAD_DIM
    f_start = 2 * qk_width + v_width
    gate_start = f_start + f_width
    assert vd == 2 * HEAD_DIM and w_in.shape[1] == gate_start + 2 * d
    assert v_width % f_width == 0 and s % (FFT_RADIX * 2 * V7X_SUBLANES) == 0

    x2 = x.reshape(n, d)
    tm = _tile(s, 1024)
    vf, h, ft = _proj_norm(x2, g_mix, w_in, 2 * qk_width, v_width + f_width, seq=s, f_col=v_width,
                           f_width=f_width, tm=_tile(s, 512), tn=1024)
    qk, wa_bf, wf_bf, wo_bf = _proj_rope(h, w_in, 0, 2 * qk_width, _rope_tables(s),
                                         (w_attn_branch, w_four_branch, w_out), seq=s, q_width=qk_width,
                                         tm=tm, tn=qk_width)
    gates, wg_bf, wu_bf, wd_bf = _proj_gates(h, w_in, gate_start, 2 * d, w_gate, w_up, w_down, tm=tm, tn=1024)

    lamv = jnp.stack([lambda_q1, lambda_k1, lambda_q2, lambda_k2]).astype(F32)
    o = _attention(qk.reshape(b, s, -1), vf.reshape(b, s, -1), lamv, g_subln, n_heads=n_heads, tq=256,
                   n_steps=min(16, s // 256))

    ma, twc, tws, cd = _fourier_tables(s)
    fm = _fourier(ft, ma, twc, tws, cd)

    x1, h2 = _merge(o.reshape(n, v_width), fm.reshape(n, f_width), gates, x2, wa_bf, wf_bf, wo_bf, g_ffn,
                    tm=_tile(n, 512))

    out = _ffn(h2, x1, wg_bf, wu_bf, wd_bf, g_final, tm=_tile(n, 1024), tf=512)
    return out.reshape(b, s, d)
```

```python
import functools
import math
from typing import NamedTuple

import jax
import jax.numpy as jnp
from jax import lax
from jax.experimental import pallas as pl
from jax.experimental.pallas import tpu as pltpu

F32 = jnp.float32
BF16 = jnp.bfloat16

HEAD_DIM = 128
ROT_DIM = HEAD_DIM // 4
ROPE_THETA = 500000.0
LAMBDA_INIT = 0.8 - 0.6 * math.exp(-0.3 * 0)
FOURIER_GROUP_DIM = 128
RMS_EPS = 1e-5
FFT_RADIX = 16

V7X_LANES = 128
V7X_SUBLANES = 8
V7X_VMEM_LIMIT_BYTES = 58 * 1024 * 1024


def _cparams(sem):
    return pltpu.CompilerParams(dimension_semantics=sem, vmem_limit_bytes=V7X_VMEM_LIMIT_BYTES)


def _proj_rope_kernel(h_ref, w_ref, cos_ref, sa_ref, sb_ref, *refs):
    n_side = len(refs) // 2
    z_ref = refs[n_side]
    for src, dst in zip(refs[:n_side], refs[n_side + 1:]):
        dst[...] = src[...].astype(BF16)

    acc = jnp.dot(h_ref[...], w_ref[...].astype(BF16), preferred_element_type=F32)
    tn = acc.shape[1]
    rep = tn // HEAD_DIM
    c = jnp.concatenate([cos_ref[0]] * rep, axis=1)
    sa = jnp.concatenate([sa_ref[0]] * rep, axis=1)
    sb = jnp.concatenate([sb_ref[0]] * rep, axis=1)
    half = ROT_DIM // 2
    r = acc * c + pltpu.roll(acc, tn - half, 1) * sa + pltpu.roll(acc, half, 1) * sb
    z_ref[...] = r.astype(BF16)


def _proj_norm_kernel(x_ref, g_ref, *refs, f_col):
    w_refs, (z_ref, h_ref, ft_ref, fs_ref) = refs[:-4], refs[-4:]
    x = x_ref[...]
    ms = jnp.mean(x * x, axis=-1, keepdims=True)
    h = (x * lax.rsqrt(ms + RMS_EPS) * g_ref[...]).astype(BF16)
    h_ref[...] = h
    tm, tn = x.shape[0], w_refs[0].shape[1]
    _, radix, rows, fw = ft_ref.shape
    for k, w_ref in enumerate(w_refs):
        res = jnp.dot(h, w_ref[...].astype(BF16), preferred_element_type=F32)
        z_ref[:, k * tn:(k + 1) * tn] = res.astype(BF16)
        if k == f_col // tn:
            for q in range(fw // V7X_LANES):
                lo = f_col % tn + q * V7X_LANES
                fs_ref[q] = res[:, lo:lo + V7X_LANES]
    for r in range(radix):
        ft_ref[0, r] = jnp.concatenate([fs_ref[q, pl.ds(r, rows, stride=radix), :]
                                        for q in range(fw // V7X_LANES)], axis=1).astype(BF16)


def _proj_norm(x2, g, w, col_start, width, *, seq, f_col, f_width, tm, tn):
    n, d = x2.shape
    assert n % tm == 0 and width % tn == 0 and col_start % tn == 0
    assert seq % tm == 0 and tm % (16 * FFT_RADIX) == 0 and f_col % tn + f_width <= tn
    col_off = col_start // tn
    s_tiles, rows = seq // tm, tm // FFT_RADIX
    w_specs = [pl.BlockSpec((d, tn), functools.partial(lambda i, k: (0, k), k=col_off + k),
                            pipeline_mode=pl.Buffered(1)) for k in range(width // tn)]
    return pl.pallas_call(
        functools.partial(_proj_norm_kernel, f_col=f_col),
        out_shape=(jax.ShapeDtypeStruct((n, width), BF16), jax.ShapeDtypeStruct((n, d), BF16),
                   jax.ShapeDtypeStruct((n // seq, FFT_RADIX, seq // FFT_RADIX, f_width), BF16)),
        grid=(n // tm,),
        in_specs=[pl.BlockSpec((tm, d), lambda i: (i, 0)), pl.BlockSpec((1, d), lambda i: (0, 0))] + w_specs,
        out_specs=(pl.BlockSpec((tm, width), lambda i: (i, 0)), pl.BlockSpec((tm, d), lambda i: (i, 0)),
                   pl.BlockSpec((1, FFT_RADIX, rows, f_width), lambda i: (i // s_tiles, 0, i % s_tiles, 0))),
        scratch_shapes=[pltpu.VMEM((f_width // V7X_LANES, tm, V7X_LANES), F32)],
        compiler_params=_cparams(("parallel",)),
        name="proj_vf_norm",
    )(x2, g.reshape(1, d), *([w] * (width // tn)))


def _proj_sigmoid_kernel(h_ref, w_ref, wg_ref, wu_ref, wd_ref, z_ref, wg_out, wu_out, wd_out):
    wg_out[...] = wg_ref[...].astype(BF16)
    wu_out[...] = wu_ref[...].astype(BF16)

    @pl.when(pl.program_id(0) == 0)
    def _():
        wd_out[...] = wd_ref[...].astype(BF16)

    acc = jnp.dot(h_ref[...], w_ref[...].astype(BF16), preferred_element_type=F32)
    z_ref[...] = jax.nn.sigmoid(acc).astype(BF16)


def _proj_gates(h, w, col_start, width, w_gate, w_up, w_down, *, tm, tn):
    n, d = h.shape
    dff = w_gate.shape[1]
    ni, nj = n // tm, width // tn
    assert n % tm == 0 and width % tn == 0 and col_start % tn == 0
    rg, rd = d // (ni * nj), dff // ni
    assert rg * ni * nj == d and rd * ni == dff and rg % 16 == 0 and rd % 16 == 0
    col_off = col_start // tn
    step = lambda j, i: (j * ni + i, 0)
    down = lambda j, i: (jnp.where(j == 0, i, ni - 1), 0)
    return pl.pallas_call(
        _proj_sigmoid_kernel,
        out_shape=(jax.ShapeDtypeStruct((n, width), BF16), jax.ShapeDtypeStruct(w_gate.shape, BF16),
                   jax.ShapeDtypeStruct(w_up.shape, BF16), jax.ShapeDtypeStruct(w_down.shape, BF16)),
        grid=(nj, ni),
        in_specs=[pl.BlockSpec((tm, d), lambda j, i: (i, 0)), pl.BlockSpec((d, tn), lambda j, i: (0, j + col_off)),
                  pl.BlockSpec((rg, dff), step), pl.BlockSpec((rg, dff), step), pl.BlockSpec((rd, d), down)],
        out_specs=(pl.BlockSpec((tm, tn), lambda j, i: (i, j)), pl.BlockSpec((rg, dff), step),
                   pl.BlockSpec((rg, dff), step), pl.BlockSpec((rd, d), down)),
        compiler_params=_cparams(("arbitrary", "arbitrary")),
        name="proj_gates",
    )(h, w, w_gate, w_up, w_down)


def _proj_rope(h, w, col_start, width, rope, side_weights, *, seq, q_width, tm, tn):
    n, d = h.shape
    assert n % tm == 0 and width % tn == 0 and col_start % tn == 0
    assert seq % tm == 0 and q_width % tn == 0 and tn % HEAD_DIM == 0
    col_off = col_start // tn
    s_tiles, q_tiles = seq // tm, q_width // tn
    table = pl.BlockSpec((1, tm, HEAD_DIM), lambda j, i: (j // q_tiles, i % s_tiles, 0))
    nj, ni = width // tn, n // tm
    side_specs = []
    for sw in side_weights:
        rows = sw.shape[0] // (nj * ni)
        assert rows * nj * ni == sw.shape[0] and rows % 16 == 0
        side_specs.append(pl.BlockSpec((rows, sw.shape[1]), lambda j, i: (j * ni + i, 0)))
    return pl.pallas_call(
        _proj_rope_kernel,
        out_shape=(jax.ShapeDtypeStruct((n, width), BF16),
                   *[jax.ShapeDtypeStruct(sw.shape, BF16) for sw in side_weights]),
        grid=(nj, ni),
        in_specs=[pl.BlockSpec((tm, d), lambda j, i: (i, 0)), pl.BlockSpec((d, tn), lambda j, i: (0, j + col_off)),
                  table, table, table] + side_specs,
        out_specs=(pl.BlockSpec((tm, tn), lambda j, i: (i, j)), *side_specs),
        compiler_params=_cparams(("arbitrary", "arbitrary")),
        name="proj_qk_rope",
    )(h, w, *rope, *side_weights)


_NT = (((1,), (1,)), ((), ()))


def _lane_fold(x, op, ways=4):
    n = x.shape[1] // V7X_LANES
    ways = min(ways, n)
    parts = [x[:, i * V7X_LANES:(i + 1) * V7X_LANES] for i in range(ways)]
    for i in range(ways, n):
        parts[i % ways] = op(parts[i % ways], x[:, i * V7X_LANES:(i + 1) * V7X_LANES])
    while len(parts) > 1:
        parts = [op(parts[2 * i], parts[2 * i + 1]) for i in range(len(parts) // 2)] + parts[len(parts) // 2 * 2:]
    return parts[0]


def _tile_rows(tile, tq):
    if isinstance(tile, int):
        return slice(tile * tq, (tile + 1) * tq)
    return pl.ds(pl.multiple_of(tile * tq, tq), tq)


def _attn_kernel(lamv_ref, g_ref, q_ref, k_ref, v_ref, o_ref, s_ref, p_ref, m_ref, r_ref, il_ref,
                 *, tq, n_steps):
    seq = k_ref.shape[1]
    n_tiles = seq // tq
    tk = seq // n_steps
    tr = tq // n_steps
    lv = lamv_ref[...]
    lam = (jnp.exp(jnp.sum(lv[0:1] * lv[1:2], axis=-1, keepdims=True))
           - jnp.exp(jnp.sum(lv[2:3] * lv[3:4], axis=-1, keepdims=True)) + LAMBDA_INIT)
    gain = g_ref[...] * (1.0 - LAMBDA_INIT)

    def qk_piece(tile, slot, j):
        rows = _tile_rows(tile, tq)
        cols = slice(j * tk, (j + 1) * tk)
        for c in range(2):
            qc = q_ref[0, rows, c * HEAD_DIM:(c + 1) * HEAD_DIM]
            kc = k_ref[0, cols, c * HEAD_DIM:(c + 1) * HEAD_DIM]
            s = lax.dot_general(qc, kc, _NT, preferred_element_type=F32)
            s_ref[slot, c, :, cols] = s
            mx = _lane_fold(s, jnp.maximum)
            m_ref[slot, c] = mx if j == 0 else jnp.maximum(m_ref[slot, c], mx)

    def exp_piece(slot, i, pace):
        rs = slice(i * tr, (i + 1) * tr)
        tot = []
        for c in range(2):
            mrow = jnp.broadcast_to(jnp.max(m_ref[slot, c, rs, :], axis=-1, keepdims=True), (tr, V7X_LANES))
            if pace is not None:
                mrow = mrow + 0.0 * jnp.concatenate([pace] * (tr // V7X_SUBLANES), axis=0)
            m = jnp.concatenate([mrow] * (seq // V7X_LANES), axis=1)
            p = jnp.exp2(s_ref[slot, c, rs, :] - m)
            p_ref[slot, c, rs, :] = p.astype(BF16)
            tot.append(jnp.sum(_lane_fold(p, jnp.add), axis=-1, keepdims=True))
        r_ref[slot, rs, :] = jnp.broadcast_to(lam * tot[0] / tot[1], (tr, V7X_LANES)).astype(BF16)
        il_ref[slot, rs, :] = jnp.broadcast_to(1.0 / tot[0], (tr, V7X_LANES))

    def pv_piece(slot, i, acc):
        cols = slice(i * tk, (i + 1) * tk)
        r = jnp.concatenate([r_ref[slot]] * (tk // V7X_LANES), axis=1)
        a = p_ref[slot, 0, :, cols] - r * p_ref[slot, 1, :, cols]
        d = jnp.dot(a, v_ref[0, cols, :], preferred_element_type=F32)
        return d if acc is None else acc + d

    def pv_finish(tile, slot, acc):
        o = acc * jnp.concatenate([il_ref[slot]] * (acc.shape[1] // V7X_LANES), axis=1)
        ms = jnp.mean(o * o, axis=-1, keepdims=True)
        o_ref[0, _tile_rows(tile, tq), :] = (o * lax.rsqrt(ms + RMS_EPS) * gain).astype(BF16)

    def period(p, par, do_qk, do_exp, do_pv):
        acc = None
        pace = None
        for i in range(n_steps):
            if do_qk:
                qk_piece(p, par, i)
            if do_exp:
                exp_piece(1 - par, i, pace)
            if do_pv:
                acc = pv_piece(par, i, acc)
                pace = acc[0:V7X_SUBLANES, 0:V7X_LANES]
        if do_pv:
            pv_finish(p - 2, par, acc)

    period(0, 0, True, False, False)
    period(1, 1, True, True, False)

    def body(g, carry):
        period(2 * g + 2, 0, True, True, True)
        period(2 * g + 3, 1, True, True, True)
        return carry

    lax.fori_loop(0, (n_tiles - 2) // 2, body, 0)
    period(n_tiles, 0, False, True, True)
    period(n_tiles + 1, 1, False, False, True)


def _attention(qk3, vf3, lamv, g_subln, *, n_heads, tq, n_steps):
    b, s, _ = qk3.shape
    vd = 2 * HEAD_DIM
    assert s % (2 * tq) == 0 and (s // n_steps) % V7X_LANES == 0 and (tq // n_steps) % 16 == 0
    return pl.pallas_call(
        functools.partial(_attn_kernel, tq=tq, n_steps=n_steps),
        out_shape=jax.ShapeDtypeStruct((b, s, n_heads * vd), BF16),
        grid=(b, n_heads),
        in_specs=[
            pl.BlockSpec((4, HEAD_DIM), lambda bi, h: (0, 0)),
            pl.BlockSpec((1, vd), lambda bi, h: (0, 0)),
            pl.BlockSpec((1, s, vd), lambda bi, h: (bi, 0, h)),
            pl.BlockSpec((1, s, vd), lambda bi, h: (bi, 0, n_heads + h)),
            pl.BlockSpec((1, s, vd), lambda bi, h: (bi, 0, h)),
        ],
        out_specs=pl.BlockSpec((1, s, vd), lambda bi, h: (bi, 0, h)),
        scratch_shapes=[
            pltpu.VMEM((2, 2, tq, s), F32),
            pltpu.VMEM((2, 2, tq, s), BF16),
            pltpu.VMEM((2, 2, tq, V7X_LANES), F32),
            pltpu.VMEM((2, tq, V7X_LANES), BF16),
            pltpu.VMEM((2, tq, V7X_LANES), F32),
        ],
        compiler_params=_cparams(("parallel", "parallel")),
        name="diff_attn",
    )(lamv, g_subln.reshape(1, vd), qk3, qk3, vf3)


def _cmul_const(x, wr, wi):
    xr, xi = x
    wr = 0.0 if abs(wr) < 1e-12 else wr
    wi = 0.0 if abs(wi) < 1e-12 else wi
    if wi == 0.0:
        return x if wr == 1.0 else ((-xr, -xi) if wr == -1.0 else (xr * wr, xi * wr))
    if wr == 0.0:
        return (xi, -xr) if wi == -1.0 else ((-xi, xr) if wi == 1.0 else (-xi * wi, xr * wi))
    return (xr * wr - xi * wi, xr * wi + xi * wr)


def _fft_list(xs):
    n = len(xs)
    if n == 1:
        return xs
    ev = _fft_list(xs[0::2])
    od = _fft_list(xs[1::2])
    out = [None] * n
    for k in range(n // 2):
        ang = -2.0 * math.pi * k / n
        tr, ti = _cmul_const(od[k], math.cos(ang), math.sin(ang))
        out[k] = (ev[k][0] + tr, ev[k][1] + ti)
        out[k + n // 2] = (ev[k][0] - tr, ev[k][1] - ti)
    return out


def _fourier_kernel(f_ref, ma_ref, twc_ref, tws_ref, cd_ref, out_ref, y_ref):
    _, n1, n2, c = f_ref.shape
    for r in range(n1):
        y = jnp.dot(ma_ref[...], f_ref[0, r], preferred_element_type=F32)
        y_ref[r, :, :c] = y[:n2]
        y_ref[r, :, c:] = y[n2:]

    def body(t, carry):
        rows = pl.ds(pl.multiple_of(t * V7X_SUBLANES, V7X_SUBLANES), V7X_SUBLANES)
        for lc in range(c // V7X_LANES):
            re_l = slice(lc * V7X_LANES, (lc + 1) * V7X_LANES)
            im_l = slice(c + lc * V7X_LANES, c + (lc + 1) * V7X_LANES)
            zs = []
            for r in range(n1):
                yr, yi = y_ref[r, rows, re_l], y_ref[r, rows, im_l]
                tc, ts = twc_ref[r, rows, :], tws_ref[r, rows, :]
                zs.append((yr * tc + yi * ts, yi * tc - yr * ts))
            for k2, (xr, xi) in enumerate(_fft_list(zs)):
                y_ref[k2, rows, re_l] = xr
                y_ref[k2, rows, im_l] = xi
        return carry

    lax.fori_loop(0, n2 // V7X_SUBLANES, body, 0)

    gd = FOURIER_GROUP_DIM
    for k2 in range(n1):
        xk = y_ref[k2].astype(BF16)
        parts = []
        for grp in range(c // gd):
            lhs = jnp.concatenate([xk[:, grp * gd:(grp + 1) * gd], xk[:, c + grp * gd:c + (grp + 1) * gd]], axis=1)
            parts.append(jnp.dot(lhs, cd_ref[...], preferred_element_type=F32))
        out_ref[0, k2 * n2:(k2 + 1) * n2, :] = jnp.concatenate(parts, axis=1).astype(BF16)


def _fourier(ft, ma, twc, tws, cd):
    b, n1, n2, c = ft.shape
    return pl.pallas_call(
        _fourier_kernel,
        out_shape=jax.ShapeDtypeStruct((b, n1 * n2, c), BF16),
        grid=(b,),
        in_specs=[
            pl.BlockSpec((1, n1, n2, c), lambda bi: (bi, 0, 0, 0)),
            pl.BlockSpec(ma.shape, lambda bi: (0, 0)),
            pl.BlockSpec(twc.shape, lambda bi: (0, 0, 0)),
            pl.BlockSpec(tws.shape, lambda bi: (0, 0, 0)),
            pl.BlockSpec(cd.shape, lambda bi: (0, 0)),
        ],
        out_specs=pl.BlockSpec((1, n1 * n2, c), lambda bi: (bi, 0, 0)),
        scratch_shapes=[pltpu.VMEM((n1, n2, 2 * c), F32)],
        compiler_params=_cparams(("parallel",)),
        name="fourier_mix",
    )(ft, ma, twc, tws, cd)


def _fourier_tables(seq):
    n1, n2 = FFT_RADIX, seq // FFT_RADIX
    gd = FOURIER_GROUP_DIM

    def cos_sin(n, scale):
        idx = jnp.arange(n, dtype=jnp.int32)
        ang = ((idx[:, None] * idx[None, :]) % n).astype(F32) * (2.0 * math.pi / n)
        return jnp.cos(ang) * scale, jnp.sin(ang) * scale

    c2, s2 = cos_sin(n2, seq ** -0.5)
    ma = jnp.concatenate([c2, -s2], axis=0).astype(BF16)
    cg, sg = cos_sin(gd, gd ** -0.5)
    cd = jnp.concatenate([cg, sg], axis=0).astype(BF16)
    r = jnp.arange(n1, dtype=jnp.int32)[:, None]
    k1 = jnp.arange(n2, dtype=jnp.int32)[None, :]
    ang = (r * k1).astype(F32) * (2.0 * math.pi / seq)
    twc = jnp.broadcast_to(jnp.cos(ang)[:, :, None], (n1, n2, V7X_LANES))
    tws = jnp.broadcast_to(jnp.sin(ang)[:, :, None], (n1, n2, V7X_LANES))
    return ma, twc, tws, cd


def _merge_kernel(o_ref, fm_ref, ga0_ref, ga1_ref, gf0_ref, gf1_ref, x_ref, wa_ref, wf_ref, wo_ref,
                  g_ref, x1_ref, h2_ref):
    o = o_ref[...]
    fm = fm_ref[...]
    acc = x_ref[...]
    half = ga0_ref.shape[1]
    for ci, (ga_ref, gf_ref) in enumerate(((ga0_ref, gf0_ref), (ga1_ref, gf1_ref))):
        cols = slice(ci * half, (ci + 1) * half)
        br_a = jnp.dot(o, wa_ref[:, cols], preferred_element_type=F32)
        br_f = jnp.dot(fm, wf_ref[:, cols], preferred_element_type=F32)
        m = ga_ref[...].astype(F32) * br_a + gf_ref[...].astype(F32) * br_f
        acc = acc + jnp.dot(m.astype(BF16), wo_ref[cols, :], preferred_element_type=F32)
    x1_ref[...] = acc
    ms = jnp.mean(acc * acc, axis=-1, keepdims=True)
    h2_ref[...] = (acc * lax.rsqrt(ms + RMS_EPS) * g_ref[...]).astype(BF16)


def _merge(o2, fm2, gates, x2, wa, wf, wo, g_ffn, *, tm):
    n, d = x2.shape
    half = d // 2
    assert gates.shape[1] == 2 * d and n % tm == 0
    const = lambda i: (0, 0)
    return pl.pallas_call(
        _merge_kernel,
        out_shape=(jax.ShapeDtypeStruct((n, d), F32), jax.ShapeDtypeStruct((n, d), BF16)),
        grid=(n // tm,),
        in_specs=[
            pl.BlockSpec((tm, o2.shape[1]), lambda i: (i, 0)),
            pl.BlockSpec((tm, fm2.shape[1]), lambda i: (i, 0)),
            pl.BlockSpec((tm, half), lambda i: (i, 0)),
            pl.BlockSpec((tm, half), lambda i: (i, 1)),
            pl.BlockSpec((tm, half), lambda i: (i, 2)),
            pl.BlockSpec((tm, half), lambda i: (i, 3)),
            pl.BlockSpec((tm, d), lambda i: (i, 0)),
            pl.BlockSpec(wa.shape, const, pipeline_mode=pl.Buffered(1)),
            pl.BlockSpec(wf.shape, const, pipeline_mode=pl.Buffered(1)),
            pl.BlockSpec(wo.shape, const, pipeline_mode=pl.Buffered(1)),
            pl.BlockSpec((1, d), const),
        ],
        out_specs=(pl.BlockSpec((tm, d), lambda i: (i, 0)), pl.BlockSpec((tm, d), lambda i: (i, 0))),
        compiler_params=_cparams(("parallel",)),
        name="merge_out_proj",
    )(o2, fm2, gates, gates, gates, gates, x2, wa, wf, wo, g_ffn.reshape(1, d))


def _ffn_kernel(h_ref, x1_ref, wg_ref, wu_ref, wd_ref, g_ref, out_ref):
    j = pl.program_id(1)

    @pl.when(j == 0)
    def _():
        out_ref[...] = x1_ref[...]

    h = h_ref[...]
    gate = jnp.dot(h, wg_ref[...], preferred_element_type=F32)
    up = jnp.dot(h, wu_ref[...], preferred_element_type=F32)
    act = (gate * jax.nn.sigmoid(gate) * up).astype(BF16)
    out_ref[...] += jnp.dot(act, wd_ref[...], preferred_element_type=F32)

    @pl.when(j == pl.num_programs(1) - 1)
    def _():
        y = out_ref[...]
        ms = jnp.mean(y * y, axis=-1, keepdims=True)
        out_ref[...] = y * lax.rsqrt(ms + RMS_EPS) * g_ref[...]


def _ffn(h2, x1, wg, wu, wd, g_final, *, tm, tf):
    n, d = x1.shape
    dff = wg.shape[1]
    assert n % tm == 0 and dff % tf == 0
    return pl.pallas_call(
        _ffn_kernel,
        out_shape=jax.ShapeDtypeStruct((n, d), F32),
        grid=(n // tm, dff // tf),
        in_specs=[
            pl.BlockSpec((tm, d), lambda i, j: (i, 0)),
            pl.BlockSpec((tm, d), lambda i, j: (i, 0)),
            pl.BlockSpec((d, tf), lambda i, j: (0, j)),
            pl.BlockSpec((d, tf), lambda i, j: (0, j)),
            pl.BlockSpec((tf, d), lambda i, j: (j, 0)),
            pl.BlockSpec((1, d), lambda i, j: (0, 0)),
        ],
        out_specs=pl.BlockSpec((tm, d), lambda i, j: (i, 0)),
        compiler_params=_cparams(("parallel", "arbitrary")),
        name="swiglu_ffn",
    )(h2, x1, wg, wu, wd, g_final.reshape(1, d))


def _rope_tables(seq):
    half = ROT_DIM // 2
    pos = jnp.arange(seq, dtype=F32)
    inv_freq = ROPE_THETA ** (-jnp.arange(0, ROT_DIM, 2, dtype=F32) / ROT_DIM)
    ang = pos[:, None] * inv_freq[None, :]
    cos, sin = jnp.cos(ang), jnp.sin(ang)
    ones = jnp.ones((seq, HEAD_DIM - ROT_DIM), F32)
    zeros = lambda w: jnp.zeros((seq, w), F32)
    rope_c = jnp.concatenate([cos, cos, ones], axis=1)
    rope_sa = jnp.concatenate([-sin, zeros(HEAD_DIM - half)], axis=1)
    rope_sb = jnp.concatenate([zeros(half), sin, zeros(HEAD_DIM - ROT_DIM)], axis=1)
    q_scale = math.log2(math.e) / math.sqrt(HEAD_DIM)
    return tuple(jnp.stack([t * q_scale, t]) for t in (rope_c, rope_sa, rope_sb))


def _tile(n, pref):
    t = min(n, pref)
    while n % t:
        t //= 2
    return t


class _Plan(NamedTuple):
    proj_rows: int
    proj_cols: int
    vf_rows: int
    attn_rows: int
    attn_pieces: int
    merge_rows: int
    ffn_rows: int
    ffn_cols: int


def _plan(b, s):
    n = b * s
    return _Plan(proj_rows=_tile(s, 1024), proj_cols=1024, vf_rows=_tile(s, 512), attn_rows=256,
                 attn_pieces=min(16, s // 256), merge_rows=_tile(n, 512), ffn_rows=_tile(n, 1024), ffn_cols=512)


def kernel(x, g_mix, w_in, lambda_q1, lambda_k1, lambda_q2, lambda_k2, g_subln, w_attn_branch,
           w_four_branch, w_out, g_ffn, w_gate, w_up, w_down, g_final):
    b, s, d = x.shape
    n = b * s
    v_width = w_attn_branch.shape[0]
    f_width = w_four_branch.shape[0]
    vd = g_subln.shape[0]
    n_heads = v_width // vd
    qk_width = n_heads * 2 * HEAD_DIM
    f_start = 2 * qk_width + v_width
    gate_start = f_start + f_width
    assert vd == 2 * HEAD_DIM and w_in.shape[1] == gate_start + 2 * d
    assert v_width % f_width == 0 and s % (FFT_RADIX * 2 * V7X_SUBLANES) == 0

    x2 = x.reshape(n, d)
    plan = _plan(b, s)
    vf, h, ft = _proj_norm(x2, g_mix, w_in, 2 * qk_width, v_width + f_width, seq=s, f_col=v_width,
                           f_width=f_width, tm=plan.vf_rows, tn=plan.proj_cols)
    qk, wa_bf, wf_bf, wo_bf = _proj_rope(h, w_in, 0, 2 * qk_width, _rope_tables(s),
                                         (w_attn_branch, w_four_branch, w_out), seq=s, q_width=qk_width,
                                         tm=plan.proj_rows, tn=qk_width)
    gates, wg_bf, wu_bf, wd_bf = _proj_gates(h, w_in, gate_start, 2 * d, w_gate, w_up, w_down,
                                             tm=plan.proj_rows, tn=plan.proj_cols)

    lamv = jnp.stack([lambda_q1, lambda_k1, lambda_q2, lambda_k2]).astype(F32)
    o = _attention(qk.reshape(b, s, -1), vf.reshape(b, s, -1), lamv, g_subln, n_heads=n_heads,
                   tq=plan.attn_rows, n_steps=plan.attn_pieces)

    ma, twc, tws, cd = _fourier_tables(s)
    fm = _fourier(ft, ma, twc, tws, cd)

    x1, h2 = _merge(o.reshape(n, v_width), fm.reshape(n, f_width), gates, x2, wa_bf, wf_bf, wo_bf, g_ffn,
                    tm=plan.merge_rows)

    out = _ffn(h2, x1, wg_bf, wu_bf, wd_bf, g_final, tm=plan.ffn_rows, tf=plan.ffn_cols)
    return out.reshape(b, s, d)
```

```python
import functools
import math
from typing import NamedTuple

import jax
import jax.numpy as jnp
from jax import lax
from jax.experimental import pallas as pl
from jax.experimental.pallas import tpu as pltpu

F32 = jnp.float32
BF16 = jnp.bfloat16

HEAD_DIM = 128
ROT_DIM = HEAD_DIM // 4
ROPE_THETA = 500000.0
LAMBDA_INIT = 0.8 - 0.6 * math.exp(-0.3 * 0)
FOURIER_GROUP_DIM = 128
RMS_EPS = 1e-5
FFT_RADIX = 16
EPILOGUE_SLABS = 4

V7X_LANES = 128
V7X_SUBLANES = 8
V7X_VMEM_LIMIT_BYTES = 58 * 1024 * 1024


def _cparams(sem):
    return pltpu.CompilerParams(dimension_semantics=sem, vmem_limit_bytes=V7X_VMEM_LIMIT_BYTES)


def _proj_rope_kernel(h_ref, w_ref, cos_ref, sa_ref, sb_ref, *refs):
    n_side = len(refs) // 2
    z_ref = refs[n_side]
    for src, dst in zip(refs[:n_side], refs[n_side + 1:]):
        dst[...] = src[...].astype(BF16)

    w = w_ref[...].astype(BF16)
    tn = w.shape[1]
    rep = tn // HEAD_DIM
    half = ROT_DIM // 2
    slab = h_ref.shape[0] // EPILOGUE_SLABS
    for part in range(EPILOGUE_SLABS):
        rows = slice(part * slab, (part + 1) * slab)
        acc = jnp.dot(h_ref[rows, :], w, preferred_element_type=F32)
        c = jnp.concatenate([cos_ref[0, rows, :]] * rep, axis=1)
        sa = jnp.concatenate([sa_ref[0, rows, :]] * rep, axis=1)
        sb = jnp.concatenate([sb_ref[0, rows, :]] * rep, axis=1)
        r = acc * c + pltpu.roll(acc, tn - half, 1) * sa + pltpu.roll(acc, half, 1) * sb
        z_ref[rows, :] = r.astype(BF16)


def _proj_norm_kernel(x_ref, g_ref, *refs, f_col):
    w_refs, (z_ref, h_ref, ft_ref, fs_ref) = refs[:-4], refs[-4:]
    x = x_ref[...]
    ms = jnp.mean(x * x, axis=-1, keepdims=True)
    h = (x * lax.rsqrt(ms + RMS_EPS) * g_ref[...]).astype(BF16)
    h_ref[...] = h
    tm, tn = x.shape[0], w_refs[0].shape[1]
    _, radix, rows, fw = ft_ref.shape
    for k, w_ref in enumerate(w_refs):
        res = jnp.dot(h, w_ref[...].astype(BF16), preferred_element_type=F32)
        z_ref[:, k * tn:(k + 1) * tn] = res.astype(BF16)
        if k == f_col // tn:
            for q in range(fw // V7X_LANES):
                lo = f_col % tn + q * V7X_LANES
                fs_ref[q] = res[:, lo:lo + V7X_LANES]
    for r in range(radix):
        ft_ref[0, r] = jnp.concatenate([fs_ref[q, pl.ds(r, rows, stride=radix), :]
                                        for q in range(fw // V7X_LANES)], axis=1).astype(BF16)


def _proj_norm(x2, g, w, col_start, width, *, seq, f_col, f_width, tm, tn):
    n, d = x2.shape
    assert n % tm == 0 and width % tn == 0 and col_start % tn == 0
    assert seq % tm == 0 and tm % (16 * FFT_RADIX) == 0 and f_col % tn + f_width <= tn
    col_off = col_start // tn
    s_tiles, rows = seq // tm, tm // FFT_RADIX
    w_specs = [pl.BlockSpec((d, tn), functools.partial(lambda i, k: (0, k), k=col_off + k),
                            pipeline_mode=pl.Buffered(1)) for k in range(width // tn)]
    return pl.pallas_call(
        functools.partial(_proj_norm_kernel, f_col=f_col),
        out_shape=(jax.ShapeDtypeStruct((n, width), BF16), jax.ShapeDtypeStruct((n, d), BF16),
                   jax.ShapeDtypeStruct((n // seq, FFT_RADIX, seq // FFT_RADIX, f_width), BF16)),
        grid=(n // tm,),
        in_specs=[pl.BlockSpec((tm, d), lambda i: (i, 0)), pl.BlockSpec((1, d), lambda i: (0, 0))] + w_specs,
        out_specs=(pl.BlockSpec((tm, width), lambda i: (i, 0)), pl.BlockSpec((tm, d), lambda i: (i, 0)),
                   pl.BlockSpec((1, FFT_RADIX, rows, f_width), lambda i: (i // s_tiles, 0, i % s_tiles, 0))),
        scratch_shapes=[pltpu.VMEM((f_width // V7X_LANES, tm, V7X_LANES), F32)],
        compiler_params=_cparams(("parallel",)),
        name="proj_vf_norm",
    )(x2, g.reshape(1, d), *([w] * (width // tn)))


def _proj_sigmoid_kernel(h_ref, w_ref, wg_ref, wu_ref, wd_ref, z_ref, wg_out, wu_out, wd_out):
    wg_out[...] = wg_ref[...].astype(BF16)
    wu_out[...] = wu_ref[...].astype(BF16)

    @pl.when(pl.program_id(0) == 0)
    def _():
        wd_out[...] = wd_ref[...].astype(BF16)

    w = w_ref[...].astype(BF16)
    slab = h_ref.shape[0] // EPILOGUE_SLABS
    for part in range(EPILOGUE_SLABS):
        rows = slice(part * slab, (part + 1) * slab)
        acc = jnp.dot(h_ref[rows, :], w, preferred_element_type=F32)
        z_ref[rows, :] = jax.nn.sigmoid(acc).astype(BF16)


def _proj_gates(h, w, col_start, width, w_gate, w_up, w_down, *, tm, tn):
    n, d = h.shape
    dff = w_gate.shape[1]
    ni, nj = n // tm, width // tn
    assert n % tm == 0 and width % tn == 0 and col_start % tn == 0
    rg, rd = d // (ni * nj), dff // ni
    assert rg * ni * nj == d and rd * ni == dff and rg % 16 == 0 and rd % 16 == 0
    col_off = col_start // tn
    step = lambda j, i: (j * ni + i, 0)
    down = lambda j, i: (jnp.where(j == 0, i, ni - 1), 0)
    return pl.pallas_call(
        _proj_sigmoid_kernel,
        out_shape=(jax.ShapeDtypeStruct((n, width), BF16), jax.ShapeDtypeStruct(w_gate.shape, BF16),
                   jax.ShapeDtypeStruct(w_up.shape, BF16), jax.ShapeDtypeStruct(w_down.shape, BF16)),
        grid=(nj, ni),
        in_specs=[pl.BlockSpec((tm, d), lambda j, i: (i, 0)), pl.BlockSpec((d, tn), lambda j, i: (0, j + col_off)),
                  pl.BlockSpec((rg, dff), step), pl.BlockSpec((rg, dff), step), pl.BlockSpec((rd, d), down)],
        out_specs=(pl.BlockSpec((tm, tn), lambda j, i: (i, j)), pl.BlockSpec((rg, dff), step),
                   pl.BlockSpec((rg, dff), step), pl.BlockSpec((rd, d), down)),
        compiler_params=_cparams(("arbitrary", "arbitrary")),
        name="proj_gates",
    )(h, w, w_gate, w_up, w_down)


def _proj_rope(h, w, col_start, width, rope, side_weights, *, seq, q_width, tm, tn):
    n, d = h.shape
    assert n % tm == 0 and width % tn == 0 and col_start % tn == 0
    assert seq % tm == 0 and q_width % tn == 0 and tn % HEAD_DIM == 0
    col_off = col_start // tn
    s_tiles, q_tiles = seq // tm, q_width // tn
    table = pl.BlockSpec((1, tm, HEAD_DIM), lambda j, i: (j // q_tiles, i % s_tiles, 0))
    nj, ni = width // tn, n // tm
    side_specs = []
    for sw in side_weights:
        rows = sw.shape[0] // (nj * ni)
        assert rows * nj * ni == sw.shape[0] and rows % 16 == 0
        side_specs.append(pl.BlockSpec((rows, sw.shape[1]), lambda j, i: (j * ni + i, 0)))
    return pl.pallas_call(
        _proj_rope_kernel,
        out_shape=(jax.ShapeDtypeStruct((n, width), BF16),
                   *[jax.ShapeDtypeStruct(sw.shape, BF16) for sw in side_weights]),
        grid=(nj, ni),
        in_specs=[pl.BlockSpec((tm, d), lambda j, i: (i, 0)), pl.BlockSpec((d, tn), lambda j, i: (0, j + col_off)),
                  table, table, table] + side_specs,
        out_specs=(pl.BlockSpec((tm, tn), lambda j, i: (i, j)), *side_specs),
        compiler_params=_cparams(("arbitrary", "arbitrary")),
        name="proj_qk_rope",
    )(h, w, *rope, *side_weights)


_NT = (((1,), (1,)), ((), ()))


def _lane_fold(x, op, ways=4):
    n = x.shape[1] // V7X_LANES
    ways = min(ways, n)
    parts = [x[:, i * V7X_LANES:(i + 1) * V7X_LANES] for i in range(ways)]
    for i in range(ways, n):
        parts[i % ways] = op(parts[i % ways], x[:, i * V7X_LANES:(i + 1) * V7X_LANES])
    while len(parts) > 1:
        parts = [op(parts[2 * i], parts[2 * i + 1]) for i in range(len(parts) // 2)] + parts[len(parts) // 2 * 2:]
    return parts[0]


def _tile_rows(tile, tq):
    if isinstance(tile, int):
        return slice(tile * tq, (tile + 1) * tq)
    return pl.ds(pl.multiple_of(tile * tq, tq), tq)


def _attn_kernel(lamv_ref, g_ref, q_ref, k_ref, v_ref, o_ref, s_ref, p_ref, m_ref, r_ref, il_ref,
                 *, tq, n_steps):
    seq = k_ref.shape[1]
    n_tiles = seq // tq
    tk = seq // n_steps
    tr = tq // n_steps
    lv = lamv_ref[...]
    lam = (jnp.exp(jnp.sum(lv[0:1] * lv[1:2], axis=-1, keepdims=True))
           - jnp.exp(jnp.sum(lv[2:3] * lv[3:4], axis=-1, keepdims=True)) + LAMBDA_INIT)
    gain = g_ref[...] * (1.0 - LAMBDA_INIT)

    def qk_piece(tile, slot, j):
        rows = _tile_rows(tile, tq)
        cols = slice(j * tk, (j + 1) * tk)
        for c in range(2):
            qc = q_ref[0, rows, c * HEAD_DIM:(c + 1) * HEAD_DIM]
            kc = k_ref[0, cols, c * HEAD_DIM:(c + 1) * HEAD_DIM]
            s = lax.dot_general(qc, kc, _NT, preferred_element_type=F32)
            s_ref[slot, c, :, cols] = s
            mx = _lane_fold(s, jnp.maximum)
            m_ref[slot, c] = mx if j == 0 else jnp.maximum(m_ref[slot, c], mx)

    def exp_piece(slot, i, pace):
        rs = slice(i * tr, (i + 1) * tr)
        tot = []
        for c in range(2):
            mrow = jnp.broadcast_to(jnp.max(m_ref[slot, c, rs, :], axis=-1, keepdims=True), (tr, V7X_LANES))
            if pace is not None:
                mrow = mrow + 0.0 * jnp.concatenate([pace] * (tr // V7X_SUBLANES), axis=0)
            m = jnp.concatenate([mrow] * (seq // V7X_LANES), axis=1)
            p = jnp.exp2(s_ref[slot, c, rs, :] - m)
            p_ref[slot, c, rs, :] = p.astype(BF16)
            tot.append(jnp.sum(_lane_fold(p, jnp.add), axis=-1, keepdims=True))
        r_ref[slot, rs, :] = jnp.broadcast_to(lam * tot[0] / tot[1], (tr, V7X_LANES)).astype(BF16)
        il_ref[slot, rs, :] = jnp.broadcast_to(1.0 / tot[0], (tr, V7X_LANES))

    def pv_piece(slot, i, acc):
        cols = slice(i * tk, (i + 1) * tk)
        r = jnp.concatenate([r_ref[slot]] * (tk // V7X_LANES), axis=1)
        a = p_ref[slot, 0, :, cols] - r * p_ref[slot, 1, :, cols]
        d = jnp.dot(a, v_ref[0, cols, :], preferred_element_type=F32)
        return d if acc is None else acc + d

    def pv_finish(tile, slot, acc):
        o = acc * jnp.concatenate([il_ref[slot]] * (acc.shape[1] // V7X_LANES), axis=1)
        ms = jnp.mean(o * o, axis=-1, keepdims=True)
        o_ref[0, _tile_rows(tile, tq), :] = (o * lax.rsqrt(ms + RMS_EPS) * gain).astype(BF16)

    def period(p, par, do_qk, do_exp, do_pv):
        acc = None
        pace = None
        for i in range(n_steps):
            if do_qk:
                qk_piece(p, par, i)
            if do_exp:
                exp_piece(1 - par, i, pace)
            if do_pv:
                acc = pv_piece(par, i, acc)
                pace = acc[0:V7X_SUBLANES, 0:V7X_LANES]
        if do_pv:
            pv_finish(p - 2, par, acc)

    period(0, 0, True, False, False)
    period(1, 1, True, True, False)

    def body(g, carry):
        period(2 * g + 2, 0, True, True, True)
        period(2 * g + 3, 1, True, True, True)
        return carry

    lax.fori_loop(0, (n_tiles - 2) // 2, body, 0)
    period(n_tiles, 0, False, True, True)
    period(n_tiles + 1, 1, False, False, True)


def _attention(qk3, vf3, lamv, g_subln, *, n_heads, tq, n_steps):
    b, s, _ = qk3.shape
    vd = 2 * HEAD_DIM
    assert s % (2 * tq) == 0 and (s // n_steps) % V7X_LANES == 0 and (tq // n_steps) % 16 == 0
    return pl.pallas_call(
        functools.partial(_attn_kernel, tq=tq, n_steps=n_steps),
        out_shape=jax.ShapeDtypeStruct((b, s, n_heads * vd), BF16),
        grid=(b, n_heads),
        in_specs=[
            pl.BlockSpec((4, HEAD_DIM), lambda bi, h: (0, 0)),
            pl.BlockSpec((1, vd), lambda bi, h: (0, 0)),
            pl.BlockSpec((1, s, vd), lambda bi, h: (bi, 0, h)),
            pl.BlockSpec((1, s, vd), lambda bi, h: (bi, 0, n_heads + h)),
            pl.BlockSpec((1, s, vd), lambda bi, h: (bi, 0, h)),
        ],
        out_specs=pl.BlockSpec((1, s, vd), lambda bi, h: (bi, 0, h)),
        scratch_shapes=[
            pltpu.VMEM((2, 2, tq, s), F32),
            pltpu.VMEM((2, 2, tq, s), BF16),
            pltpu.VMEM((2, 2, tq, V7X_LANES), F32),
            pltpu.VMEM((2, tq, V7X_LANES), BF16),
            pltpu.VMEM((2, tq, V7X_LANES), F32),
        ],
        compiler_params=_cparams(("parallel", "parallel")),
        name="diff_attn",
    )(lamv, g_subln.reshape(1, vd), qk3, qk3, vf3)


def _cmul_const(x, wr, wi):
    xr, xi = x
    wr = 0.0 if abs(wr) < 1e-12 else wr
    wi = 0.0 if abs(wi) < 1e-12 else wi
    if wi == 0.0:
        return x if wr == 1.0 else ((-xr, -xi) if wr == -1.0 else (xr * wr, xi * wr))
    if wr == 0.0:
        return (xi, -xr) if wi == -1.0 else ((-xi, xr) if wi == 1.0 else (-xi * wi, xr * wi))
    return (xr * wr - xi * wi, xr * wi + xi * wr)


def _fft_list(xs):
    n = len(xs)
    if n == 1:
        return xs
    ev = _fft_list(xs[0::2])
    od = _fft_list(xs[1::2])
    out = [None] * n
    for k in range(n // 2):
        ang = -2.0 * math.pi * k / n
        tr, ti = _cmul_const(od[k], math.cos(ang), math.sin(ang))
        out[k] = (ev[k][0] + tr, ev[k][1] + ti)
        out[k + n // 2] = (ev[k][0] - tr, ev[k][1] - ti)
    return out


def _fourier_kernel(f_ref, ma_ref, twc_ref, tws_ref, cd_ref, out_ref, y_ref):
    _, n1, n2, c = f_ref.shape
    for r in range(n1):
        y = jnp.dot(ma_ref[...], f_ref[0, r], preferred_element_type=F32)
        y_ref[r, :, :c] = y[:n2]
        y_ref[r, :, c:] = y[n2:]

    def body(t, carry):
        rows = pl.ds(pl.multiple_of(t * V7X_SUBLANES, V7X_SUBLANES), V7X_SUBLANES)
        for lc in range(c // V7X_LANES):
            re_l = slice(lc * V7X_LANES, (lc + 1) * V7X_LANES)
            im_l = slice(c + lc * V7X_LANES, c + (lc + 1) * V7X_LANES)
            zs = []
            for r in range(n1):
                yr, yi = y_ref[r, rows, re_l], y_ref[r, rows, im_l]
                tc, ts = twc_ref[r, rows, :], tws_ref[r, rows, :]
                zs.append((yr * tc + yi * ts, yi * tc - yr * ts))
            for k2, (xr, xi) in enumerate(_fft_list(zs)):
                y_ref[k2, rows, re_l] = xr
                y_ref[k2, rows, im_l] = xi
        return carry

    lax.fori_loop(0, n2 // V7X_SUBLANES, body, 0)

    gd = FOURIER_GROUP_DIM
    for k2 in range(n1):
        xk = y_ref[k2].astype(BF16)
        parts = []
        for grp in range(c // gd):
            lhs = jnp.concatenate([xk[:, grp * gd:(grp + 1) * gd], xk[:, c + grp * gd:c + (grp + 1) * gd]], axis=1)
            parts.append(jnp.dot(lhs, cd_ref[...], preferred_element_type=F32))
        out_ref[0, k2 * n2:(k2 + 1) * n2, :] = jnp.concatenate(parts, axis=1).astype(BF16)


def _fourier(ft, ma, twc, tws, cd):
    b, n1, n2, c = ft.shape
    return pl.pallas_call(
        _fourier_kernel,
        out_shape=jax.ShapeDtypeStruct((b, n1 * n2, c), BF16),
        grid=(b,),
        in_specs=[
            pl.BlockSpec((1, n1, n2, c), lambda bi: (bi, 0, 0, 0)),
            pl.BlockSpec(ma.shape, lambda bi: (0, 0)),
            pl.BlockSpec(twc.shape, lambda bi: (0, 0, 0)),
            pl.BlockSpec(tws.shape, lambda bi: (0, 0, 0)),
            pl.BlockSpec(cd.shape, lambda bi: (0, 0)),
        ],
        out_specs=pl.BlockSpec((1, n1 * n2, c), lambda bi: (bi, 0, 0)),
        scratch_shapes=[pltpu.VMEM((n1, n2, 2 * c), F32)],
        compiler_params=_cparams(("parallel",)),
        name="fourier_mix",
    )(ft, ma, twc, tws, cd)


def _fourier_tables(seq):
    n1, n2 = FFT_RADIX, seq // FFT_RADIX
    gd = FOURIER_GROUP_DIM

    def cos_sin(n, scale):
        idx = jnp.arange(n, dtype=jnp.int32)
        ang = ((idx[:, None] * idx[None, :]) % n).astype(F32) * (2.0 * math.pi / n)
        return jnp.cos(ang) * scale, jnp.sin(ang) * scale

    c2, s2 = cos_sin(n2, seq ** -0.5)
    ma = jnp.concatenate([c2, -s2], axis=0).astype(BF16)
    cg, sg = cos_sin(gd, gd ** -0.5)
    cd = jnp.concatenate([cg, sg], axis=0).astype(BF16)
    r = jnp.arange(n1, dtype=jnp.int32)[:, None]
    k1 = jnp.arange(n2, dtype=jnp.int32)[None, :]
    ang = (r * k1).astype(F32) * (2.0 * math.pi / seq)
    twc = jnp.broadcast_to(jnp.cos(ang)[:, :, None], (n1, n2, V7X_LANES))
    tws = jnp.broadcast_to(jnp.sin(ang)[:, :, None], (n1, n2, V7X_LANES))
    return ma, twc, tws, cd


def _merge_kernel(o_ref, fm_ref, ga0_ref, ga1_ref, gf0_ref, gf1_ref, x_ref, wa_ref, wf_ref, wo_ref,
                  g_ref, x1_ref, h2_ref):
    o = o_ref[...]
    fm = fm_ref[...]
    acc = x_ref[...]
    half = ga0_ref.shape[1]
    for ci, (ga_ref, gf_ref) in enumerate(((ga0_ref, gf0_ref), (ga1_ref, gf1_ref))):
        cols = slice(ci * half, (ci + 1) * half)
        br_a = jnp.dot(o, wa_ref[:, cols], preferred_element_type=F32)
        br_f = jnp.dot(fm, wf_ref[:, cols], preferred_element_type=F32)
        m = ga_ref[...].astype(F32) * br_a + gf_ref[...].astype(F32) * br_f
        acc = acc + jnp.dot(m.astype(BF16), wo_ref[cols, :], preferred_element_type=F32)
    x1_ref[...] = acc
    ms = jnp.mean(acc * acc, axis=-1, keepdims=True)
    h2_ref[...] = (acc * lax.rsqrt(ms + RMS_EPS) * g_ref[...]).astype(BF16)


def _merge(o2, fm2, gates, x2, wa, wf, wo, g_ffn, *, tm):
    n, d = x2.shape
    half = d // 2
    assert gates.shape[1] == 2 * d and n % tm == 0
    const = lambda i: (0, 0)
    return pl.pallas_call(
        _merge_kernel,
        out_shape=(jax.ShapeDtypeStruct((n, d), F32), jax.ShapeDtypeStruct((n, d), BF16)),
        grid=(n // tm,),
        in_specs=[
            pl.BlockSpec((tm, o2.shape[1]), lambda i: (i, 0)),
            pl.BlockSpec((tm, fm2.shape[1]), lambda i: (i, 0)),
            pl.BlockSpec((tm, half), lambda i: (i, 0)),
            pl.BlockSpec((tm, half), lambda i: (i, 1)),
            pl.BlockSpec((tm, half), lambda i: (i, 2)),
            pl.BlockSpec((tm, half), lambda i: (i, 3)),
            pl.BlockSpec((tm, d), lambda i: (i, 0)),
            pl.BlockSpec(wa.shape, const, pipeline_mode=pl.Buffered(1)),
            pl.BlockSpec(wf.shape, const, pipeline_mode=pl.Buffered(1)),
            pl.BlockSpec(wo.shape, const, pipeline_mode=pl.Buffered(1)),
            pl.BlockSpec((1, d), const),
        ],
        out_specs=(pl.BlockSpec((tm, d), lambda i: (i, 0)), pl.BlockSpec((tm, d), lambda i: (i, 0))),
        compiler_params=_cparams(("parallel",)),
        name="merge_out_proj",
    )(o2, fm2, gates, gates, gates, gates, x2, wa, wf, wo, g_ffn.reshape(1, d))


def _ffn_kernel(h_ref, x1_ref, wg_ref, wu_ref, wd_ref, g_ref, out_ref):
    j = pl.program_id(1)

    @pl.when(j == 0)
    def _():
        out_ref[...] = x1_ref[...]

    h = h_ref[...]
    gate = jnp.dot(h, wg_ref[...], preferred_element_type=F32)
    up = jnp.dot(h, wu_ref[...], preferred_element_type=F32)
    act = (gate * jax.nn.sigmoid(gate) * up).astype(BF16)
    out_ref[...] += jnp.dot(act, wd_ref[...], preferred_element_type=F32)

    @pl.when(j == pl.num_programs(1) - 1)
    def _():
        y = out_ref[...]
        ms = jnp.mean(y * y, axis=-1, keepdims=True)
        out_ref[...] = y * lax.rsqrt(ms + RMS_EPS) * g_ref[...]


def _ffn(h2, x1, wg, wu, wd, g_final, *, tm, tf):
    n, d = x1.shape
    dff = wg.shape[1]
    assert n % tm == 0 and dff % tf == 0
    return pl.pallas_call(
        _ffn_kernel,
        out_shape=jax.ShapeDtypeStruct((n, d), F32),
        grid=(n // tm, dff // tf),
        in_specs=[
            pl.BlockSpec((tm, d), lambda i, j: (i, 0)),
            pl.BlockSpec((tm, d), lambda i, j: (i, 0)),
            pl.BlockSpec((d, tf), lambda i, j: (0, j)),
            pl.BlockSpec((d, tf), lambda i, j: (0, j)),
            pl.BlockSpec((tf, d), lambda i, j: (j, 0)),
            pl.BlockSpec((1, d), lambda i, j: (0, 0)),
        ],
        out_specs=pl.BlockSpec((tm, d), lambda i, j: (i, 0)),
        compiler_params=_cparams(("parallel", "arbitrary")),
        name="swiglu_ffn",
    )(h2, x1, wg, wu, wd, g_final.reshape(1, d))


def _rope_tables(seq):
    half = ROT_DIM // 2
    pos = jnp.arange(seq, dtype=F32)
    inv_freq = ROPE_THETA ** (-jnp.arange(0, ROT_DIM, 2, dtype=F32) / ROT_DIM)
    ang = pos[:, None] * inv_freq[None, :]
    cos, sin = jnp.cos(ang), jnp.sin(ang)
    ones = jnp.ones((seq, HEAD_DIM - ROT_DIM), F32)
    zeros = lambda w: jnp.zeros((seq, w), F32)
    rope_c = jnp.concatenate([cos, cos, ones], axis=1)
    rope_sa = jnp.concatenate([-sin, zeros(HEAD_DIM - half)], axis=1)
    rope_sb = jnp.concatenate([zeros(half), sin, zeros(HEAD_DIM - ROT_DIM)], axis=1)
    q_scale = math.log2(math.e) / math.sqrt(HEAD_DIM)
    return tuple(jnp.stack([t * q_scale, t]) for t in (rope_c, rope_sa, rope_sb))


def _tile(n, pref):
    t = min(n, pref)
    while n % t:
        t //= 2
    return t


class _Plan(NamedTuple):
    proj_rows: int
    proj_cols: int
    vf_rows: int
    attn_rows: int
    attn_pieces: int
    merge_rows: int
    ffn_rows: int
    ffn_cols: int


def _plan(b, s):
    n = b * s
    return _Plan(proj_rows=_tile(s, 1024), proj_cols=1024, vf_rows=_tile(s, 512), attn_rows=256,
                 attn_pieces=min(16, s // 256), merge_rows=_tile(n, 512), ffn_rows=_tile(n, 1024), ffn_cols=512)


def kernel(x, g_mix, w_in, lambda_q1, lambda_k1, lambda_q2, lambda_k2, g_subln, w_attn_branch,
           w_four_branch, w_out, g_ffn, w_gate, w_up, w_down, g_final):
    b, s, d = x.shape
    n = b * s
    v_width = w_attn_branch.shape[0]
    f_width = w_four_branch.shape[0]
    vd = g_subln.shape[0]
    n_heads = v_width // vd
    qk_width = n_heads * 2 * HEAD_DIM
    f_start = 2 * qk_width + v_width
    gate_start = f_start + f_width
    assert vd == 2 * HEAD_DIM and w_in.shape[1] == gate_start + 2 * d
    assert v_width % f_width == 0 and s % (FFT_RADIX * 2 * V7X_SUBLANES) == 0

    x2 = x.reshape(n, d)
    plan = _plan(b, s)
    vf, h, ft = _proj_norm(x2, g_mix, w_in, 2 * qk_width, v_width + f_width, seq=s, f_col=v_width,
                           f_width=f_width, tm=plan.vf_rows, tn=plan.proj_cols)
    qk, wa_bf, wf_bf, wo_bf = _proj_rope(h, w_in, 0, 2 * qk_width, _rope_tables(s),
                                         (w_attn_branch, w_four_branch, w_out), seq=s, q_width=qk_width,
                                         tm=plan.proj_rows, tn=qk_width)
    gates, wg_bf, wu_bf, wd_bf = _proj_gates(h, w_in, gate_start, 2 * d, w_gate, w_up, w_down,
                                             tm=plan.proj_rows, tn=plan.proj_cols)

    lamv = jnp.stack([lambda_q1, lambda_k1, lambda_q2, lambda_k2]).astype(F32)
    o = _attention(qk.reshape(b, s, -1), vf.reshape(b, s, -1), lamv, g_subln, n_heads=n_heads,
                   tq=plan.attn_rows, n_steps=plan.attn_pieces)

    ma, twc, tws, cd = _fourier_tables(s)
    fm = _fourier(ft, ma, twc, tws, cd)

    x1, h2 = _merge(o.reshape(n, v_width), fm.reshape(n, f_width), gates, x2, wa_bf, wf_bf, wo_bf, g_ffn,
                    tm=plan.merge_rows)

    out = _ffn(h2, x1, wg_bf, wu_bf, wd_bf, g_final, tm=plan.ffn_rows, tf=plan.ffn_cols)
    return out.reshape(b, s, d)
```
